```python
import math
import jax
import jax.numpy as jnp
from jax import lax
import numpy as np

D_MODEL = 2048
BATCH = 4
SEQ = 4096
DEPTH = 2

N_MIXERS = 2
N_A = (DEPTH + 1) // 2
N_B = DEPTH // 2
RMS_EPS = 1e-6

MLA_HEADS = 16
MLA_Q_RANK = 512
MLA_KV_RANK = 512
MLA_NOPE = 128
MLA_ROPE = 64
MLA_V = 128
MLA_IN_DIM = MLA_Q_RANK + MLA_KV_RANK + MLA_ROPE
ROPE_THETA = 10000.0
Q_BLOCK = 128

SSD_INNER = 2 * D_MODEL
SSD_HEADDIM = 64
SSD_HEADS = SSD_INNER // SSD_HEADDIM
SSD_GROUPS = 8
SSD_STATE = 128
SSD_CONV = 4
SSD_CHUNK = 256
SSD_GN = SSD_GROUPS * SSD_STATE
SSD_CONV_DIM = SSD_INNER + 2 * SSD_GN
SSD_IN_DIM = 2 * SSD_INNER + 2 * SSD_GN + SSD_HEADS

FFN_DIM = 7168
N_EXPERTS = 8
TOP_K = 2
MOE_BLOCK = 512

kernel_name = "hybrid_mla_ssd_moe_trunk"


def rmsnorm(x, g):
    xf = x.astype(jnp.float32)
    r = lax.rsqrt(jnp.mean(xf * xf, axis=-1, keepdims=True) + RMS_EPS)
    return (xf * r).astype(x.dtype) * g


def apply_rope(t, cos, sin):
    half = t.shape[-1] // 2
    t1, t2 = t[..., :half], t[..., half:]
    return jnp.concatenate([t1 * cos - t2 * sin, t1 * sin + t2 * cos], axis=-1)


def mla_mixer(h, positions, w_in, q_norm, kv_norm, w_uq, w_ukv, w_o):
    bsz, s, _ = h.shape
    c = h @ w_in
    c_q = c[..., :MLA_Q_RANK]
    c_kv = c[..., MLA_Q_RANK:MLA_Q_RANK + MLA_KV_RANK]
    k_rope = c[..., MLA_Q_RANK + MLA_KV_RANK:]
    q = (rmsnorm(c_q, q_norm) @ w_uq).reshape(bsz, s, MLA_HEADS, MLA_NOPE + MLA_ROPE)
    q_nope, q_rope = q[..., :MLA_NOPE], q[..., MLA_NOPE:]
    kv = (rmsnorm(c_kv, kv_norm) @ w_ukv).reshape(bsz, s, MLA_HEADS, MLA_NOPE + MLA_V)
    k_nope, v = kv[..., :MLA_NOPE], kv[..., MLA_NOPE:]
    inv_freq = ROPE_THETA ** (-jnp.arange(0, MLA_ROPE, 2, dtype=jnp.float32) / MLA_ROPE)
    ang = positions.astype(jnp.float32)[..., None] * inv_freq
    cos, sin = jnp.cos(ang).astype(h.dtype), jnp.sin(ang).astype(h.dtype)
    q_rope = apply_rope(q_rope, cos[:, :, None], sin[:, :, None])
    k_rope = apply_rope(k_rope, cos, sin)
    scale = (MLA_NOPE + MLA_ROPE) ** -0.5
    nq = s // Q_BLOCK
    qn_b = q_nope.reshape(bsz, nq, Q_BLOCK, MLA_HEADS, MLA_NOPE).swapaxes(0, 1)
    qr_b = q_rope.reshape(bsz, nq, Q_BLOCK, MLA_HEADS, MLA_ROPE).swapaxes(0, 1)
    key_pos = jnp.arange(s)

    def attend(args):
        qn, qr, qi = args
        sc = jnp.einsum('bqhd,bkhd->bhqk', qn, k_nope) + jnp.einsum('bqhr,bkr->bhqk', qr, k_rope)
        sc = sc.astype(jnp.float32) * scale
        q_pos = qi * Q_BLOCK + jnp.arange(Q_BLOCK)
        sc = jnp.where(key_pos[None, :] <= q_pos[:, None], sc, -jnp.inf)
        p = jax.nn.softmax(sc, axis=-1).astype(v.dtype)
        return jnp.einsum('bhqk,bkhd->bqhd', p, v)

    o = lax.map(attend, (qn_b, qr_b, jnp.arange(nq)))
    o = o.swapaxes(0, 1).reshape(bsz, s, MLA_HEADS * MLA_V)
    return o @ w_o


def causal_dwconv(u, w, b):
    k = w.shape[0]
    out = lax.conv_general_dilated(u, w[:, None, :], window_strides=(1,), padding=[(k - 1, 0)],
                                   dimension_numbers=('NWC', 'WIO', 'NWC'),
                                   feature_group_count=u.shape[-1])
    return out + b


def ssd_chunked_scan(x, dt, a_neg, bm, cm):
    bsz, s, _, p = x.shape
    hg = SSD_HEADS // SSD_GROUPS
    L = SSD_CHUNK
    pad = (-s) % L
    if pad:
        padf = lambda t: jnp.pad(t, [(0, 0), (0, pad)] + [(0, 0)] * (t.ndim - 2))
        x, dt, bm, cm = padf(x), padf(dt), padf(bm), padf(cm)
    nc = (s + pad) // L
    to_chunks = lambda t: t.reshape((bsz, nc, L) + t.shape[2:]).swapaxes(0, 1)
    xc = to_chunks(x.reshape(bsz, s + pad, SSD_GROUPS, hg, p))
    dtc = to_chunks(dt.reshape(bsz, s + pad, SSD_GROUPS, hg))
    ac = dtc * a_neg.reshape(SSD_GROUPS, hg)
    bc, cc = to_chunks(bm), to_chunks(cm)
    causal = jnp.tril(jnp.ones((L, L), dtype=bool))

    def step(state, inp):
        xk, dtk, ak, bk, ck = inp
        acum = jnp.cumsum(ak, axis=1)
        seg = acum[:, :, None] - acum[:, None, :]
        decay = jnp.exp(jnp.where(causal[None, :, :, None, None], seg, -jnp.inf))
        cb = jnp.einsum('blgn,bsgn->blsg', ck, bk)
        xdt = xk * dtk[..., None]
        y_intra = jnp.einsum('blsg,blsgh,bsghp->blghp', cb, decay, xdt)
        y_state = jnp.einsum('blgn,bghpn->blghp', ck, state) * jnp.exp(acum)[..., None]
        to_end = jnp.exp(acum[:, -1:] - acum)
        new_state = state * jnp.exp(acum[:, -1])[..., None, None] + \
            jnp.einsum('bsgh,bsghp,bsgn->bghpn', to_end, xdt, bk)
        return new_state, y_intra + y_state

    state0 = jnp.zeros((bsz, SSD_GROUPS, hg, p, SSD_STATE), jnp.float32)
    _, ys = lax.scan(step, state0, (xc, dtc, ac, bc, cc))
    return ys.swapaxes(0, 1).reshape(bsz, nc * L, SSD_HEADS, p)[:, :s]


def ssd_mixer(h, w_in, conv_w, conv_b, dt_bias, a_log, d_skip, norm_g, w_o):
    bsz, s, _ = h.shape
    zxbcdt = h @ w_in
    z = zxbcdt[..., :SSD_INNER]
    xbc = zxbcdt[..., SSD_INNER:SSD_INNER + SSD_CONV_DIM]
    dt_raw = zxbcdt[..., SSD_INNER + SSD_CONV_DIM:]
    xbc = jax.nn.silu(causal_dwconv(xbc, conv_w, conv_b))
    xs = xbc[..., :SSD_INNER].reshape(bsz, s, SSD_HEADS, SSD_HEADDIM).astype(jnp.float32)
    bm = xbc[..., SSD_INNER:SSD_INNER + SSD_GN].reshape(bsz, s, SSD_GROUPS, SSD_STATE).astype(jnp.float32)
    cm = xbc[..., SSD_INNER + SSD_GN:].reshape(bsz, s, SSD_GROUPS, SSD_STATE).astype(jnp.float32)
    dt = jax.nn.softplus(dt_raw.astype(jnp.float32) + dt_bias.astype(jnp.float32))
    a_neg = -jnp.exp(a_log.astype(jnp.float32))
    y = ssd_chunked_scan(xs, dt, a_neg, bm, cm)
    y = y + xs * d_skip.astype(jnp.float32)[:, None]
    yg = (y.reshape(bsz, s, SSD_INNER) * jax.nn.silu(z.astype(jnp.float32)))
    yg = yg.reshape(bsz, s, SSD_GROUPS, SSD_INNER // SSD_GROUPS)
    yg = yg * lax.rsqrt(jnp.mean(yg * yg, axis=-1, keepdims=True) + RMS_EPS)
    y_out = yg.reshape(bsz, s, SSD_INNER).astype(h.dtype) * norm_g
    return y_out @ w_o


def swiglu(h, w_gate, w_up, w_down):
    return (jax.nn.silu(h @ w_gate) * (h @ w_up)) @ w_down


def moe_swiglu(h, router, w_gate, w_up, w_down):
    bsz, s, d = h.shape
    t = bsz * s
    xf = h.reshape(t, d)
    probs = jax.nn.softmax((xf @ router).astype(jnp.float32), axis=-1)
    top_p, top_i = lax.top_k(probs, TOP_K)
    gates = top_p / jnp.sum(top_p, axis=-1, keepdims=True)
    n_assign = t * TOP_K
    e_flat = top_i.reshape(-1).astype(jnp.int32)
    tok_flat = jnp.arange(n_assign, dtype=jnp.int32) // TOP_K
    g_flat = gates.reshape(-1)
    order = jnp.argsort(e_flat * n_assign + jnp.arange(n_assign, dtype=jnp.int32))
    e_sorted = e_flat[order]
    counts = jnp.bincount(e_flat, length=N_EXPERTS)
    starts = jnp.cumsum(counts) - counts
    padded = ((counts + MOE_BLOCK - 1) // MOE_BLOCK) * MOE_BLOCK
    pad_ends = jnp.cumsum(padded)
    pad_starts = pad_ends - padded
    dest = pad_starts[e_sorted] + (jnp.arange(n_assign) - starts[e_sorted])
    n_blocks = -(-n_assign // MOE_BLOCK) + N_EXPERTS
    cap = n_blocks * MOE_BLOCK
    slot_tok = jnp.full((cap,), t, jnp.int32).at[dest].set(tok_flat[order])
    slot_gate = jnp.zeros((cap,), jnp.float32).at[dest].set(g_flat[order])
    block_expert = jnp.minimum(jnp.searchsorted(pad_ends, jnp.arange(n_blocks) * MOE_BLOCK, side='right'),
                               N_EXPERTS - 1)
    x_pad = jnp.concatenate([xf, jnp.zeros((1, d), xf.dtype)], axis=0)
    xb = x_pad[slot_tok].reshape(n_blocks, MOE_BLOCK, d)

    def expert_block(args):
        xblk, e = args
        return swiglu(xblk, w_gate[e], w_up[e], w_down[e])

    yb = lax.map(expert_block, (xb, block_expert)).reshape(cap, d)
    y = jax.ops.segment_sum(yb * slot_gate[:, None].astype(yb.dtype), slot_tok, num_segments=t + 1)[:t]
    return y.reshape(bsz, s, d)


def setup_inputs(seed: int = 0) -> dict:
    key = jax.random.key(seed)
    ks = jax.random.split(key, 26)
    nrm = lambda k, shape, sc: jax.random.normal(k, shape, jnp.float32) * sc
    dt0 = jnp.exp(jax.random.uniform(ks[11], (N_B, SSD_HEADS), jnp.float32, math.log(1e-3), math.log(1e-1)))
    return {
        "x": nrm(ks[0], (BATCH, SEQ, D_MODEL), 1.0),
        "positions": jnp.arange(SEQ, dtype=jnp.int32)[None, :] + jax.random.randint(ks[1], (BATCH, 1), 0, 1024, dtype=jnp.int32),
        "mla_w_in": nrm(ks[2], (N_A, D_MODEL, MLA_IN_DIM), D_MODEL ** -0.5),
        "mla_q_norm": 1.0 + nrm(ks[3], (N_A, MLA_Q_RANK), 0.02),
        "mla_kv_norm": 1.0 + nrm(ks[4], (N_A, MLA_KV_RANK), 0.02),
        "mla_w_uq": nrm(ks[5], (N_A, MLA_Q_RANK, MLA_HEADS * (MLA_NOPE + MLA_ROPE)), MLA_Q_RANK ** -0.5),
        "mla_w_ukv": nrm(ks[6], (N_A, MLA_KV_RANK, MLA_HEADS * (MLA_NOPE + MLA_V)), MLA_KV_RANK ** -0.5),
        "mla_w_o": nrm(ks[7], (N_A, MLA_HEADS * MLA_V, D_MODEL), (MLA_HEADS * MLA_V) ** -0.5),
        "ssd_w_in": nrm(ks[8], (N_B, D_MODEL, SSD_IN_DIM), D_MODEL ** -0.5),
        "ssd_conv_w": nrm(ks[9], (N_B, SSD_CONV, SSD_CONV_DIM), SSD_CONV ** -0.5),
        "ssd_conv_b": nrm(ks[10], (N_B, SSD_CONV_DIM), 0.02),
        "ssd_dt_bias": dt0 + jnp.log(-jnp.expm1(-dt0)),
        "ssd_a_log": jnp.log(jax.random.uniform(ks[12], (N_B, SSD_HEADS), jnp.float32, 1.0, 16.0)),
        "ssd_d": 1.0 + nrm(ks[13], (N_B, SSD_HEADS), 0.1),
        "ssd_norm": 1.0 + nrm(ks[14], (N_B, SSD_INNER), 0.02),
        "ssd_w_o": nrm(ks[15], (N_B, SSD_INNER, D_MODEL), SSD_INNER ** -0.5),
        "ffn_w_gate": nrm(ks[16], (N_A, D_MODEL, FFN_DIM), D_MODEL ** -0.5),
        "ffn_w_up": nrm(ks[17], (N_A, D_MODEL, FFN_DIM), D_MODEL ** -0.5),
        "ffn_w_down": nrm(ks[18], (N_A, FFN_DIM, D_MODEL), FFN_DIM ** -0.5),
        "moe_router": nrm(ks[19], (N_B, D_MODEL, N_EXPERTS), D_MODEL ** -0.5),
        "moe_w_gate": nrm(ks[20], (N_B, N_EXPERTS, D_MODEL, FFN_DIM), D_MODEL ** -0.5),
        "moe_w_up": nrm(ks[21], (N_B, N_EXPERTS, D_MODEL, FFN_DIM), D_MODEL ** -0.5),
        "moe_w_down": nrm(ks[22], (N_B, N_EXPERTS, FFN_DIM, D_MODEL), FFN_DIM ** -0.5),
        "norm_mix": 1.0 + nrm(ks[23], (DEPTH, D_MODEL), 0.02),
        "norm_ffn": 1.0 + nrm(ks[24], (DEPTH, D_MODEL), 0.02),
        "norm_final": 1.0 + nrm(ks[25], (D_MODEL,), 0.02),
    }


def reference(x, positions, mla_w_in, mla_q_norm, mla_kv_norm, mla_w_uq, mla_w_ukv, mla_w_o,
              ssd_w_in, ssd_conv_w, ssd_conv_b, ssd_dt_bias, ssd_a_log, ssd_d, ssd_norm, ssd_w_o,
              ffn_w_gate, ffn_w_up, ffn_w_down, moe_router, moe_w_gate, moe_w_up, moe_w_down,
              norm_mix, norm_ffn, norm_final):
    for i in range(DEPTH):
        j = i // 2
        hn = rmsnorm(x, norm_mix[i])
        if i % N_MIXERS == 0:
            mix = mla_mixer(hn, positions, mla_w_in[j], mla_q_norm[j], mla_kv_norm[j],
                            mla_w_uq[j], mla_w_ukv[j], mla_w_o[j])
        else:
            mix = ssd_mixer(hn, ssd_w_in[j], ssd_conv_w[j], ssd_conv_b[j], ssd_dt_bias[j],
                            ssd_a_log[j], ssd_d[j], ssd_norm[j], ssd_w_o[j])
        x = x + mix
        hn = rmsnorm(x, norm_ffn[i])
        if i % 2 == 0:
            ffn = swiglu(hn, ffn_w_gate[j], ffn_w_up[j], ffn_w_down[j])
        else:
            ffn = moe_swiglu(hn, moe_router[j], moe_w_gate[j], moe_w_up[j], moe_w_down[j])
        x = x + ffn
    return rmsnorm(x, norm_final)
```

```python
import functools

import jax
import jax.numpy as jnp
from jax import lax
from jax.experimental import pallas as pl
from jax.experimental.pallas import tpu as pltpu

F32 = jnp.float32
BF16 = jnp.bfloat16
U32 = jnp.uint32
I32 = jnp.int32

RMS_EPS = 1e-6
MLA_HEADS = 16
MLA_Q_RANK = 512
MLA_KV_RANK = 512
MLA_NOPE = 128
MLA_ROPE = 64
MLA_V = 128
ROPE_THETA = 10000.0
SSD_HEADDIM = 64
SSD_GROUPS = 8
SSD_STATE = 128
SSD_CONV = 4
SSD_CHUNK = 256
N_EXPERTS = 8
TOP_K = 2
MOE_ROWS = 512

LANES = 128
QK_WIDTH = 2 * LANES
VMEM_LIMIT_BYTES = 56 * 1024 * 1024
HIGHEST = lax.Precision.HIGHEST


def _cparams(*sem):
    return pltpu.CompilerParams(dimension_semantics=sem, vmem_limit_bytes=VMEM_LIMIT_BYTES)


def _blk(n, pref):
    b = min(n, pref)
    assert n % b == 0, (n, pref)
    return b


def _resident(shape, index_map):
    return pl.BlockSpec(shape, index_map, pipeline_mode=pl.Buffered(1))


def _rms(xf, g):
    r = lax.rsqrt(jnp.mean(xf * xf, axis=-1, keepdims=True) + RMS_EPS)
    return xf * r * g


def _silu(x):
    return x * (1.0 / (1.0 + jnp.exp(-x)))


def _dot(a, b):
    return jnp.dot(a, b, preferred_element_type=F32)


def _dot_nt(a, b, precision=None):
    return lax.dot_general(a, b, (((1,), (1,)), ((), ())), preferred_element_type=F32, precision=precision)


def _dot_tn(a, b):
    return lax.dot_general(a, b, (((0,), (0,)), ((), ())), preferred_element_type=F32)


def _mla_in_kernel(x_ref, g_ref, w_ref, qn_ref, kvn_ref, cr_ref, sr_ref, cq_ref, ckv_ref, kr_ref):
    hn = _rms(x_ref[...], g_ref[...]).astype(BF16)
    c = _dot(hn, w_ref[...])
    cq_ref[...] = _rms(c[:, :MLA_Q_RANK], qn_ref[...]).astype(BF16)
    ckv_ref[...] = _rms(c[:, MLA_Q_RANK:MLA_Q_RANK + MLA_KV_RANK], kvn_ref[...]).astype(BF16)
    o = MLA_Q_RANK + MLA_KV_RANK
    kr_ref[...] = (c[:, o:o + LANES] * cr_ref[...] + c[:, o + LANES:o + 2 * LANES] * sr_ref[...]).astype(BF16)


def _mla_in(x, g, w_in_p, q_norm, kv_norm, cr, sr):
    t, d = x.shape
    bm = _blk(t, 512)
    wn = w_in_p.shape[1]
    row = lambda i: (i, 0)
    fixed = lambda i: (0, 0)
    return pl.pallas_call(
        _mla_in_kernel,
        grid=(t // bm,),
        in_specs=[
            pl.BlockSpec((bm, d), row),
            _resident((1, d), fixed),
            _resident((d, wn), fixed),
            _resident((1, MLA_Q_RANK), fixed),
            _resident((1, MLA_KV_RANK), fixed),
            pl.BlockSpec((bm, LANES), row),
            pl.BlockSpec((bm, LANES), row),
        ],
        out_specs=[
            pl.BlockSpec((bm, MLA_Q_RANK), row),
            pl.BlockSpec((bm, MLA_KV_RANK), row),
            pl.BlockSpec((bm, LANES), row),
        ],
        out_shape=[
            jax.ShapeDtypeStruct((t, MLA_Q_RANK), BF16),
            jax.ShapeDtypeStruct((t, MLA_KV_RANK), BF16),
            jax.ShapeDtypeStruct((t, LANES), BF16),
        ],
        compiler_params=_cparams("parallel"),
        name="mla_in",
    )(x, g, w_in_p, q_norm, kv_norm, cr, sr)


def _mla_up_kernel(cq_ref, ckv_ref, kr_ref, wq_ref, wqs_ref, wk_ref, wv_ref, cr_ref, sr_ref,
                   q_ref, k_ref, v_ref, *, scale):
    cq = cq_ref[...]
    ckv = ckv_ref[...]
    a = _dot(cq, wq_ref[...])
    b = _dot(cq, wqs_ref[...])
    kn = _dot(ckv, wk_ref[...])
    v_ref[...] = _dot(ckv, wv_ref[...]).astype(BF16)
    cr = cr_ref[...] * scale
    sr = sr_ref[...] * scale
    kr = kr_ref[...]
    for h in range(MLA_HEADS):
        q0 = h * QK_WIDTH
        q_ref[:, q0:q0 + LANES] = (a[:, q0:q0 + LANES] * scale).astype(BF16)
        q_ref[:, q0 + LANES:q0 + QK_WIDTH] = (
            a[:, q0 + LANES:q0 + QK_WIDTH] * cr + b[:, h * LANES:(h + 1) * LANES] * sr).astype(BF16)
        k_ref[:, q0:q0 + LANES] = kn[:, h * LANES:(h + 1) * LANES].astype(BF16)
        k_ref[:, q0 + LANES:q0 + QK_WIDTH] = kr


def _mla_up(cq, ckv, kr, wq, wqs, wk, wv, cr, sr):
    t = cq.shape[0]
    bm = _blk(t, 256)
    row = lambda i: (i, 0)
    fixed = lambda i: (0, 0)
    hq = MLA_HEADS * QK_WIDTH
    hv = MLA_HEADS * MLA_V
    scale = float((MLA_NOPE + MLA_ROPE) ** -0.5)
    return pl.pallas_call(
        functools.partial(_mla_up_kernel, scale=scale),
        grid=(t // bm,),
        in_specs=[
            pl.BlockSpec((bm, MLA_Q_RANK), row),
            pl.BlockSpec((bm, MLA_KV_RANK), row),
            pl.BlockSpec((bm, LANES), row),
            _resident(wq.shape, fixed),
            _resident(wqs.shape, fixed),
            _resident(wk.shape, fixed),
            _resident(wv.shape, fixed),
            pl.BlockSpec((bm, LANES), row),
            pl.BlockSpec((bm, LANES), row),
        ],
        out_specs=[
            pl.BlockSpec((bm, hq), row),
            pl.BlockSpec((bm, hq), row),
            pl.BlockSpec((bm, hv), row),
        ],
        out_shape=[
            jax.ShapeDtypeStruct((t, hq), BF16),
            jax.ShapeDtypeStruct((t, hq), BF16),
            jax.ShapeDtypeStruct((t, hv), BF16),
        ],
        compiler_params=_cparams("parallel"),
        name="mla_up",
    )(cq, ckv, kr, wq, wqs, wk, wv, cr, sr)


def _flash_kernel(q_ref, k_ref, v_ref, o_ref, m_scr, l_scr, acc_scr, *, tq, tk):
    i = pl.program_id(2)
    j = pl.program_id(3)
    last_j = (i * tq + tq - 1) // tk

    @pl.when(j == 0)
    def _():
        m_scr[...] = jnp.full(m_scr.shape, -jnp.inf, F32)
        l_scr[...] = jnp.zeros(l_scr.shape, F32)
        acc_scr[...] = jnp.zeros(acc_scr.shape, F32)

    def step(masked):
        s = _dot_nt(q_ref[0], k_ref[0])
        if masked:
            qpos = i * tq + lax.broadcasted_iota(I32, (tq, tk), 0)
            kpos = j * tk + lax.broadcasted_iota(I32, (tq, tk), 1)
            s = jnp.where(kpos <= qpos, s, -jnp.inf)
        m_prev = m_scr[...]
        m_new = jnp.maximum(m_prev, jnp.max(s, axis=1, keepdims=True))
        alpha = jnp.exp(m_prev - m_new)
        p = jnp.exp(s - m_new)
        l_scr[...] = alpha * l_scr[...] + jnp.sum(p, axis=1, keepdims=True)
        acc_scr[...] = alpha * acc_scr[...] + _dot(p.astype(BF16), v_ref[0])
        m_scr[...] = m_new

    @pl.when((j + 1) * tk <= i * tq + 1)
    def _():
        step(False)

    @pl.when(((j + 1) * tk > i * tq + 1) & (j <= last_j))
    def _():
        step(True)

    @pl.when(j == last_j)
    def _():
        o_ref[0] = (acc_scr[...] / l_scr[...]).astype(o_ref.dtype)


def _flash(q, k, v, bsz, s):
    tq = _blk(s, 1024)
    tk = _blk(s, 512)
    kv_map = lambda b, h, i, j: (b, jnp.minimum(j, (i * tq + tq - 1) // tk), h)
    return pl.pallas_call(
        functools.partial(_flash_kernel, tq=tq, tk=tk),
        grid=(bsz, MLA_HEADS, s // tq, s // tk),
        in_specs=[
            pl.BlockSpec((1, tq, QK_WIDTH), lambda b, h, i, j: (b, i, h)),
            pl.BlockSpec((1, tk, QK_WIDTH), kv_map),
            pl.BlockSpec((1, tk, MLA_V), kv_map),
        ],
        out_specs=pl.BlockSpec((1, tq, MLA_V), lambda b, h, i, j: (b, i, h)),
        out_shape=jax.ShapeDtypeStruct((bsz, s, MLA_HEADS * MLA_V), BF16),
        scratch_shapes=[
            pltpu.VMEM((tq, 1), F32),
            pltpu.VMEM((tq, 1), F32),
            pltpu.VMEM((tq, MLA_V), F32),
        ],
        compiler_params=_cparams("parallel", "parallel", "parallel", "arbitrary"),
        name="mla_flash",
    )(q, k, v)


def _proj_res_norm_kernel(a_ref, w_ref, x_ref, g_ref, xo_ref, hn_ref):
    k = pl.program_id(1)

    @pl.when(k == 0)
    def _():
        xo_ref[...] = x_ref[...]

    xo_ref[...] += _dot(a_ref[...], w_ref[...])

    @pl.when(k == pl.num_programs(1) - 1)
    def _():
        hn_ref[...] = _rms(xo_ref[...], g_ref[...]).astype(hn_ref.dtype)


def _proj_res_norm(a, w, x, g):
    t, kdim = a.shape
    d = w.shape[1]
    bm = _blk(t, 512)
    bk = _blk(kdim, 1024)
    return pl.pallas_call(
        _proj_res_norm_kernel,
        grid=(t // bm, kdim // bk),
        in_specs=[
            pl.BlockSpec((bm, bk), lambda i, k: (i, k)),
            pl.BlockSpec((bk, d), lambda i, k: (k, 0)),
            pl.BlockSpec((bm, d), lambda i, k: (i, 0)),
            _resident((1, d), lambda i, k: (0, 0)),
        ],
        out_specs=[
            pl.BlockSpec((bm, d), lambda i, k: (i, 0)),
            pl.BlockSpec((bm, d), lambda i, k: (i, 0)),
        ],
        out_shape=[
            jax.ShapeDtypeStruct((t, d), F32),
            jax.ShapeDtypeStruct((t, d), BF16),
        ],
        compiler_params=_cparams("parallel", "arbitrary"),
        name="proj_res_norm",
    )(a, w, x, g)


def _ffn_kernel(hn_ref, x_ref, wg_ref, wu_ref, wd_ref, o_ref):
    f = pl.program_id(1)

    @pl.when(f == 0)
    def _():
        o_ref[...] = x_ref[...]

    hn = hn_ref[...]
    h = (_silu(_dot(hn, wg_ref[...])) * _dot(hn, wu_ref[...])).astype(BF16)
    o_ref[...] += _dot(h, wd_ref[...])


def _ffn(hn, x, wg, wu, wd):
    t, d = x.shape
    fdim = wg.shape[1]
    bm = _blk(t, 512)
    bf = _blk(fdim, 512)
    return pl.pallas_call(
        _ffn_kernel,
        grid=(t // bm, fdim // bf),
        in_specs=[
            pl.BlockSpec((bm, d), lambda i, f: (i, 0)),
            pl.BlockSpec((bm, d), lambda i, f: (i, 0)),
            pl.BlockSpec((d, bf), lambda i, f: (0, f)),
            pl.BlockSpec((d, bf), lambda i, f: (0, f)),
            pl.BlockSpec((bf, d), lambda i, f: (f, 0)),
        ],
        out_specs=pl.BlockSpec((bm, d), lambda i, f: (i, 0)),
        out_shape=jax.ShapeDtypeStruct((t, d), F32),
        compiler_params=_cparams("parallel", "arbitrary"),
        name="dense_ffn",
    )(hn, x, wg, wu, wd)


def _ssd_in_kernel(x_ref, g_ref, w_ref, wdt_ref, dtb_ref, zx_ref, dtt_ref, hn_scr):
    j = pl.program_id(1)

    @pl.when(j == 0)
    def _():
        hn = _rms(x_ref[...], g_ref[...]).astype(BF16)
        hn_scr[...] = hn
        raw = _dot_nt(wdt_ref[...], hn) + dtb_ref[...]
        dtt_ref[...] = jnp.maximum(raw, 0.0) + jnp.log1p(jnp.exp(-jnp.abs(raw)))

    zx_ref[...] = _dot(hn_scr[...], w_ref[...]).astype(zx_ref.dtype)


def _ssd_in(x, g, w_main, w_dt_t, dt_bias_col):
    t, d = x.shape
    n = w_main.shape[1]
    bm = _blk(t, 1024)
    bn = _blk(n, 1024)
    hp = w_dt_t.shape[0]
    return pl.pallas_call(
        _ssd_in_kernel,
        grid=(t // bm, n // bn),
        in_specs=[
            pl.BlockSpec((bm, d), lambda i, j: (i, 0)),
            _resident((1, d), lambda i, j: (0, 0)),
            pl.BlockSpec((d, bn), lambda i, j: (0, j)),
            _resident((hp, d), lambda i, j: (0, 0)),
            _resident((hp, 1), lambda i, j: (0, 0)),
        ],
        out_specs=[
            pl.BlockSpec((bm, bn), lambda i, j: (i, j)),
            pl.BlockSpec((hp, bm), lambda i, j: (0, i)),
        ],
        out_shape=[
            jax.ShapeDtypeStruct((t, n), BF16),
            jax.ShapeDtypeStruct((hp, t), F32),
        ],
        scratch_shapes=[pltpu.VMEM((bm, d), BF16)],
        compiler_params=_cparams("parallel", "arbitrary"),
        name="ssd_in",
    )(x, g, w_main, w_dt_t, dt_bias_col)


def _ssd_kernel(z_ref, x_ref, b_ref, c_ref, dtt_ref, alog_ref, cwx_ref, cwb_ref, cwc_ref,
                cbx_ref, cbb_ref, cbc_ref, dsk_ref, ng_ref, y_ref, ubuf, state_scr, *, L, gw):
    c_idx = pl.program_id(2)
    hg = gw // SSD_HEADDIM
    n = SSD_STATE
    tail = 8

    @pl.when(c_idx == 0)
    def _():
        ubuf[0:tail, :] = jnp.zeros((tail, ubuf.shape[1]), F32)
        state_scr[...] = jnp.zeros(state_scr.shape, F32)

    ubuf[tail:tail + L, 0:gw] = x_ref[...].astype(F32)
    ubuf[tail:tail + L, gw:gw + n] = b_ref[...].astype(F32)
    ubuf[tail:tail + L, gw + n:gw + 2 * n] = c_ref[...].astype(F32)

    cw = jnp.concatenate([cwx_ref[...], cwb_ref[...], cwc_ref[...]], axis=1)
    cb = jnp.concatenate([cbx_ref[...], cbb_ref[...], cbc_ref[...]], axis=1)
    conv = cb + cw[SSD_CONV - 1:SSD_CONV, :] * ubuf[tail:tail + L, :]
    for kk in range(SSD_CONV - 1):
        off = tail - (SSD_CONV - 1) + kk
        conv = conv + cw[kk:kk + 1, :] * ubuf[off:off + L, :]
    ubuf[0:tail, :] = ubuf[L:L + tail, :]
    u = _silu(conv)
    xc = u[:, 0:gw]
    bc = u[:, gw:gw + n].astype(BF16)
    cc = u[:, gw + n:gw + 2 * n].astype(BF16)

    dtt = dtt_ref[...]
    a_t = dtt * (-jnp.exp(alog_ref[...]))
    ri = lax.broadcasted_iota(I32, (L, L), 0)
    ci = lax.broadcasted_iota(I32, (L, L), 1)
    causal = ri >= ci
    tril = causal.astype(F32)
    eye = (ri == ci).astype(F32)
    acum_row = _dot_nt(a_t, tril, precision=HIGHEST)
    acum_col = _dot_nt(tril, a_t, precision=HIGHEST)
    dt_col = _dot_nt(eye, dtt, precision=HIGHEST)

    hrow = lax.broadcasted_iota(I32, (hg, gw), 0)
    hlane = lax.broadcasted_iota(I32, (hg, gw), 1) // SSD_HEADDIM
    expand = (hrow == hlane).astype(F32)
    dt_e = jnp.dot(dt_col, expand, precision=HIGHEST, preferred_element_type=F32)
    acum_e = jnp.dot(acum_col, expand, precision=HIGHEST, preferred_element_type=F32)
    last_e = acum_e[L - 1:L, :]

    xdt = xc * dt_e
    cbm = _dot_nt(cc, bc)

    st = state_scr[...]
    y = _dot(cc, st.astype(BF16)) * jnp.exp(acum_e) + xc * dsk_ref[...]

    lane = lax.broadcasted_iota(I32, (L, LANES), 1)
    pieces = []
    for pair in range(hg // 2):
        xp = xdt[:, pair * LANES:(pair + 1) * LANES]
        yp = None
        for sub in range(2):
            jh = 2 * pair + sub
            seg = acum_col[:, jh:jh + 1] - acum_row[jh:jh + 1, :]
            m = (cbm * jnp.exp(jnp.where(causal, seg, -jnp.inf))).astype(BF16)
            keep = (lane // SSD_HEADDIM) == sub
            d = _dot(m, jnp.where(keep, xp, 0.0).astype(BF16))
            yp = d if yp is None else yp + d
        pieces.append(yp)
    y = y + jnp.concatenate(pieces, axis=1)

    xw = (xdt * jnp.exp(last_e - acum_e)).astype(BF16)
    state_scr[...] = st * jnp.exp(last_e) + _dot_tn(bc, xw)

    zf = z_ref[...].astype(F32)
    yg = y * _silu(zf)
    yg = yg * lax.rsqrt(jnp.mean(yg * yg, axis=-1, keepdims=True) + RMS_EPS)
    y_ref[...] = (yg * ng_ref[...]).astype(y_ref.dtype)


def _ssd_core(zx, dtt, a_log_col, conv_w, conv_b, d_exp, norm_g, bsz, s, inner):
    t = zx.shape[0]
    L = SSD_CHUNK
    assert s % L == 0
    nc = s // L
    gw = inner // SSD_GROUPS
    hg = gw // SSD_HEADDIM
    n = SSD_STATE
    gn = SSD_GROUPS * n
    xb0 = inner // gw
    bb0 = (2 * inner) // n
    cb0 = (2 * inner + gn) // n
    cwb0 = inner // n
    cwc0 = (inner + gn) // n
    rowi = lambda b, g, c: b * nc + c
    return pl.pallas_call(
        functools.partial(_ssd_kernel, L=L, gw=gw),
        grid=(bsz, SSD_GROUPS, nc),
        in_specs=[
            pl.BlockSpec((L, gw), lambda b, g, c: (rowi(b, g, c), g)),
            pl.BlockSpec((L, gw), lambda b, g, c: (rowi(b, g, c), xb0 + g)),
            pl.BlockSpec((L, n), lambda b, g, c: (rowi(b, g, c), bb0 + g)),
            pl.BlockSpec((L, n), lambda b, g, c: (rowi(b, g, c), cb0 + g)),
            pl.BlockSpec((hg, L), lambda b, g, c: (g, rowi(b, g, c))),
            pl.BlockSpec((hg, 1), lambda b, g, c: (g, 0)),
            pl.BlockSpec((SSD_CONV, gw), lambda b, g, c: (0, g)),
            pl.BlockSpec((SSD_CONV, n), lambda b, g, c: (0, cwb0 + g)),
            pl.BlockSpec((SSD_CONV, n), lambda b, g, c: (0, cwc0 + g)),
            pl.BlockSpec((1, gw), lambda b, g, c: (0, g)),
            pl.BlockSpec((1, n), lambda b, g, c: (0, cwb0 + g)),
            pl.BlockSpec((1, n), lambda b, g, c: (0, cwc0 + g)),
            pl.BlockSpec((1, gw), lambda b, g, c: (0, g)),
            pl.BlockSpec((1, gw), lambda b, g, c: (0, g)),
        ],
        out_specs=pl.BlockSpec((L, gw), lambda b, g, c: (rowi(b, g, c), g)),
        out_shape=jax.ShapeDtypeStruct((t, inner), BF16),
        scratch_shapes=[
            pltpu.VMEM((L + 8, gw + 2 * n), F32),
            pltpu.VMEM((n, gw), F32),
        ],
        compiler_params=_cparams("parallel", "parallel", "arbitrary"),
        name="ssd_core",
    )(zx, zx, zx, zx, dtt, a_log_col, conv_w, conv_w, conv_w, conv_b, conv_b, conv_b, d_exp, norm_g)


def _ssd_out_router_kernel(a_ref, w_ref, x_ref, g_ref, r_ref, xo_ref, hnp_ref, gate_ref, meta_ref, cnt_ref,
                           run_scr):
    i = pl.program_id(0)
    k = pl.program_id(1)
    bm, d = xo_ref.shape

    @pl.when((i == 0) & (k == 0))
    def _():
        run_scr[...] = jnp.zeros(run_scr.shape, F32)

    @pl.when(k == 0)
    def _():
        xo_ref[...] = x_ref[...]

    xo_ref[...] += _dot(a_ref[...], w_ref[...])

    @pl.when(k == pl.num_programs(1) - 1)
    def _():
        hn = _rms(xo_ref[...], g_ref[...])
        bits = pltpu.bitcast(hn.astype(BF16).astype(F32), U32)
        hnp_ref[...] = (bits[:, d // 2:] & jnp.uint32(0xFFFF0000)) | (bits[:, :d // 2] >> 16)

        logits = jnp.dot(hn, r_ref[...], precision=HIGHEST, preferred_element_type=F32)
        lane = lax.broadcasted_iota(I32, logits.shape, 1)
        logits = jnp.where(lane < N_EXPERTS, logits, -jnp.inf)
        m1 = jnp.max(logits, axis=1, keepdims=True)
        i1 = jnp.min(jnp.where(logits == m1, lane, LANES), axis=1, keepdims=True)
        rest = jnp.where(lane == i1, -jnp.inf, logits)
        m2 = jnp.max(rest, axis=1, keepdims=True)
        i2 = jnp.min(jnp.where(rest == m2, lane, LANES), axis=1, keepdims=True)
        e21 = jnp.exp(m2 - m1)
        g1 = 1.0 / (1.0 + e21)
        g2 = e21 / (1.0 + e21)
        gate_ref[...] = jnp.where(lane == 0, g1, jnp.where(lane == 1, g2, 0.0))

        oh1 = lane == i1
        oh2 = lane == i2
        both = (oh1 | oh2).astype(BF16)
        ri = lax.broadcasted_iota(I32, (bm, bm), 0)
        ci = lax.broadcasted_iota(I32, (bm, bm), 1)
        before = _dot((ri > ci).astype(BF16), both) + run_scr[...]
        r1 = jnp.sum(jnp.where(oh1, before, 0.0), axis=1, keepdims=True)
        r2 = jnp.sum(jnp.where(oh2, before, 0.0), axis=1, keepdims=True)
        meta = jnp.where(lane == 0, i1.astype(F32),
                         jnp.where(lane == 1, i2.astype(F32),
                                   jnp.where(lane == 2, r1, jnp.where(lane == 3, r2, 0.0))))
        meta_ref[...] = meta.astype(I32)
        run_scr[...] = run_scr[...] + jnp.sum(both.astype(F32), axis=0, keepdims=True)
        cnt_ref[...] = jnp.broadcast_to(run_scr[...], cnt_ref.shape).astype(I32)


def _ssd_out_router(a, w, x, g, router_p):
    t, kdim = a.shape
    d = w.shape[1]
    bm = _blk(t, 512)
    bk = _blk(kdim, 1024)
    row = lambda i, k: (i, 0)
    return pl.pallas_call(
        _ssd_out_router_kernel,
        grid=(t // bm, kdim // bk),
        in_specs=[
            pl.BlockSpec((bm, bk), lambda i, k: (i, k)),
            pl.BlockSpec((bk, d), lambda i, k: (k, 0)),
            pl.BlockSpec((bm, d), row),
            _resident((1, d), lambda i, k: (0, 0)),
            _resident((d, LANES), lambda i, k: (0, 0)),
        ],
        out_specs=[
            pl.BlockSpec((bm, d), row),
            pl.BlockSpec((bm, d // 2), row),
            pl.BlockSpec((bm, LANES), row),
            pl.BlockSpec((bm, LANES), row),
            pl.BlockSpec((8, LANES), lambda i, k: (0, 0)),
        ],
        out_shape=[
            jax.ShapeDtypeStruct((t, d), F32),
            jax.ShapeDtypeStruct((t, d // 2), U32),
            jax.ShapeDtypeStruct((t, LANES), F32),
            jax.ShapeDtypeStruct((t, LANES), I32),
            jax.ShapeDtypeStruct((8, LANES), I32),
        ],
        scratch_shapes=[pltpu.VMEM((1, LANES), F32)],
        compiler_params=_cparams("arbitrary", "arbitrary"),
        name="ssd_out_router",
    )(a, w, x, g, router_p)


def _dispatch_kernel(pos_ref, hn_ref, xb_in_ref, xb_ref, sem):
    del xb_in_ref
    bm = hn_ref.shape[0]

    def row_copy(r, kk):
        dst = pos_ref[0, 0, TOP_K * r + kk]
        return pltpu.make_async_copy(hn_ref.at[pl.ds(r, 1)], xb_ref.at[pl.ds(dst, 1)], sem)

    def start(r, carry):
        for kk in range(TOP_K):
            row_copy(r, kk).start()
        return carry

    def wait(r, carry):
        for kk in range(TOP_K):
            row_copy(r, kk).wait()
        return carry

    lax.fori_loop(0, bm, start, 0)
    lax.fori_loop(0, bm, wait, 0)


def _dispatch(hnp, pos_blocks, cap):
    t, w = hnp.shape
    nb, _, two_bm = pos_blocks.shape
    bm = two_bm // TOP_K
    xb_init = jnp.zeros((cap, w), U32)
    return pl.pallas_call(
        _dispatch_kernel,
        grid=(nb,),
        in_specs=[
            pl.BlockSpec((1, 1, two_bm), lambda i: (i, 0, 0), memory_space=pltpu.SMEM),
            pl.BlockSpec((bm, w), lambda i: (i, 0)),
            pl.BlockSpec(memory_space=pl.ANY),
        ],
        out_specs=pl.BlockSpec(memory_space=pl.ANY),
        out_shape=jax.ShapeDtypeStruct((cap, w), U32),
        scratch_shapes=[pltpu.SemaphoreType.DMA(())],
        input_output_aliases={2: 0},
        compiler_params=_cparams("arbitrary"),
        name="moe_dispatch",
    )(pos_blocks, hnp, xb_init)


def _moe_kernel(be_ref, nu_ref, xb_ref, wg_ref, wu_ref, wd_ref, o_ref, xs_scr):
    del be_ref
    b = pl.program_id(0)
    f = pl.program_id(1)

    @pl.when(f == 0)
    def _():
        o_ref[...] = jnp.zeros(o_ref.shape, F32)

    @pl.when(b < nu_ref[0])
    def _():
        @pl.when(f == 0)
        def _():
            w = xb_ref[...]
            half = w.shape[1]
            xs_scr[:, :half] = pltpu.bitcast(w << 16, F32).astype(BF16)
            xs_scr[:, half:] = pltpu.bitcast(w & jnp.uint32(0xFFFF0000), F32).astype(BF16)

        xs = xs_scr[...]
        h = (_silu(_dot(xs, wg_ref[0])) * _dot(xs, wu_ref[0])).astype(BF16)
        o_ref[...] += _dot(h, wd_ref[0])


def _moe(xb, block_expert, n_used, wg, wu, wd, bm):
    cap, half = xb.shape
    d = 2 * half
    fdim = wg.shape[2]
    bf = _blk(fdim, 512)
    nf = fdim // bf
    nb = cap // bm

    def live(b, nu):
        return jnp.minimum(b, nu[0] - 1)

    def f_eff(b, f, nu):
        return jnp.where(b < nu[0], f, nf - 1)

    grid_spec = pltpu.PrefetchScalarGridSpec(
        num_scalar_prefetch=2,
        grid=(nb, nf),
        in_specs=[
            pl.BlockSpec((bm, half), lambda b, f, be, nu: (live(b, nu), 0)),
            pl.BlockSpec((1, d, bf), lambda b, f, be, nu: (be[live(b, nu)], 0, f_eff(b, f, nu))),
            pl.BlockSpec((1, d, bf), lambda b, f, be, nu: (be[live(b, nu)], 0, f_eff(b, f, nu))),
            pl.BlockSpec((1, bf, d), lambda b, f, be, nu: (be[live(b, nu)], f_eff(b, f, nu), 0)),
        ],
        out_specs=pl.BlockSpec((bm, d), lambda b, f, be, nu: (b, 0)),
        scratch_shapes=[pltpu.VMEM((bm, d), BF16)],
    )
    return pl.pallas_call(
        _moe_kernel,
        grid_spec=grid_spec,
        out_shape=jax.ShapeDtypeStruct((cap, d), F32),
        compiler_params=_cparams("arbitrary", "arbitrary"),
        name="moe_ffn",
    )(block_expert, n_used, xb, wg, wu, wd)


def _combine_kernel(pos_ref, x_ref, gate_ref, g_ref, yb_ref, o_ref, ybuf, sem):
    bm = x_ref.shape[0]

    def row_copy(r, kk):
        src = pos_ref[0, 0, TOP_K * r + kk]
        return pltpu.make_async_copy(yb_ref.at[pl.ds(src, 1)], ybuf.at[kk, pl.ds(r, 1)], sem)

    def start(r, carry):
        for kk in range(TOP_K):
            row_copy(r, kk).start()
        return carry

    def wait(r, carry):
        for kk in range(TOP_K):
            row_copy(r, kk).wait()
        return carry

    lax.fori_loop(0, bm, start, 0)
    lax.fori_loop(0, bm, wait, 0)
    gates = gate_ref[...]
    x = x_ref[...] + gates[:, 0:1] * ybuf[0] + gates[:, 1:2] * ybuf[1]
    o_ref[...] = _rms(x, g_ref[...])


def _combine(x, gates, g_final, yb, pos_blocks):
    t, d = x.shape
    nb, _, two_bm = pos_blocks.shape
    bm = two_bm // TOP_K
    return pl.pallas_call(
        _combine_kernel,
        grid=(nb,),
        in_specs=[
            pl.BlockSpec((1, 1, two_bm), lambda i: (i, 0, 0), memory_space=pltpu.SMEM),
            pl.BlockSpec((bm, d), lambda i: (i, 0)),
            pl.BlockSpec((bm, LANES), lambda i: (i, 0)),
            _resident((1, d), lambda i: (0, 0)),
            pl.BlockSpec(memory_space=pl.ANY),
        ],
        out_specs=pl.BlockSpec((bm, d), lambda i: (i, 0)),
        out_shape=jax.ShapeDtypeStruct((t, d), F32),
        scratch_shapes=[pltpu.VMEM((TOP_K, bm, d), F32), pltpu.SemaphoreType.DMA(())],
        compiler_params=_cparams("arbitrary"),
        name="moe_combine",
    )(pos_blocks, x, gates, g_final, yb)


def _rope_tables(positions):
    inv_freq = ROPE_THETA ** (-jnp.arange(0, MLA_ROPE, 2, dtype=F32) / MLA_ROPE)
    ang = positions.astype(F32).reshape(-1)[:, None] * inv_freq
    cos, sin = jnp.cos(ang), jnp.sin(ang)
    zeros = jnp.zeros((ang.shape[0], LANES - MLA_ROPE), F32)
    return jnp.concatenate([cos, cos, zeros], axis=1), jnp.concatenate([-sin, sin, zeros], axis=1)


def _swap_halves(w):
    half = w.shape[-1] // 2
    return jnp.concatenate([w[..., half:], w[..., :half]], axis=-1)


def _mla_weights(w_in, w_uq, w_ukv):
    d = w_in.shape[0]
    o = MLA_Q_RANK + MLA_KV_RANK
    w_kr = w_in[:, o:]
    zpad = jnp.zeros((d, LANES - MLA_ROPE), w_in.dtype)
    w_in_p = jnp.concatenate([w_in[:, :o], w_kr, zpad, _swap_halves(w_kr), zpad], axis=1).astype(BF16)
    uq = w_uq.reshape(MLA_Q_RANK, MLA_HEADS, MLA_NOPE + MLA_ROPE)
    rope = uq[:, :, MLA_NOPE:]
    zq = jnp.zeros((MLA_Q_RANK, MLA_HEADS, LANES - MLA_ROPE), w_uq.dtype)
    wq = jnp.concatenate([uq[:, :, :MLA_NOPE], rope, zq], axis=2).reshape(MLA_Q_RANK, -1).astype(BF16)
    wqs = jnp.concatenate([_swap_halves(rope), zq], axis=2).reshape(MLA_Q_RANK, -1).astype(BF16)
    ukv = w_ukv.reshape(MLA_KV_RANK, MLA_HEADS, MLA_NOPE + MLA_V)
    wk = ukv[:, :, :MLA_NOPE].reshape(MLA_KV_RANK, -1).astype(BF16)
    wv = ukv[:, :, MLA_NOPE:].reshape(MLA_KV_RANK, -1).astype(BF16)
    return w_in_p, wq, wqs, wk, wv


def kernel(x, positions, mla_w_in, mla_q_norm, mla_kv_norm, mla_w_uq, mla_w_ukv, mla_w_o, ssd_w_in, ssd_conv_w, ssd_conv_b, ssd_dt_bias, ssd_a_log, ssd_d, ssd_norm, ssd_w_o, ffn_w_gate, ffn_w_up, ffn_w_down, moe_router, moe_w_gate, moe_w_up, moe_w_down, norm_mix, norm_ffn, norm_final):
    bsz, s, d = x.shape
    t = bsz * s
    assert norm_mix.shape[0] == 2 and mla_w_in.shape[0] == 1 and ssd_w_in.shape[0] == 1
    xf = x.reshape(t, d)
    cr, sr = _rope_tables(positions)
    row = lambda v: v.reshape(1, -1)
    bf = lambda v: v.astype(BF16)

    w_in_p, wq, wqs, wk, wv = _mla_weights(mla_w_in[0], mla_w_uq[0], mla_w_ukv[0])
    cq, ckv, kr = _mla_in(xf, row(norm_mix[0]), w_in_p, row(mla_q_norm[0]), row(mla_kv_norm[0]), cr, sr)
    q, k, v = _mla_up(cq, ckv, kr, wq, wqs, wk, wv, cr, sr)
    o = _flash(q.reshape(bsz, s, -1), k.reshape(bsz, s, -1), v.reshape(bsz, s, -1), bsz, s)
    xf, hn = _proj_res_norm(o.reshape(t, -1), bf(mla_w_o[0]), xf, row(norm_ffn[0]))
    xf = _ffn(hn, xf, bf(ffn_w_gate[0]), bf(ffn_w_up[0]), bf(ffn_w_down[0]))

    inner = ssd_norm.shape[1]
    heads = inner // SSD_HEADDIM
    n_main = 2 * inner + 2 * SSD_GROUPS * SSD_STATE
    w_in = ssd_w_in[0]
    w_dt_t = bf(jnp.zeros((LANES, d), F32).at[:heads].set(w_in[:, n_main:].T))
    dtb = jnp.zeros((LANES, 1), F32).at[:heads, 0].set(ssd_dt_bias[0])
    zx, dtt = _ssd_in(xf, row(norm_mix[1]), bf(w_in[:, :n_main]), w_dt_t, dtb)
    a_log_col = jnp.zeros((LANES, 1), F32).at[:heads, 0].set(ssd_a_log[0])
    d_exp = jnp.repeat(ssd_d[0], SSD_HEADDIM).reshape(1, inner)
    y = _ssd_core(zx, dtt, a_log_col, ssd_conv_w[0], row(ssd_conv_b[0]), d_exp, row(ssd_norm[0]), bsz, s, inner)
    router_p = jnp.zeros((d, LANES), F32).at[:, :N_EXPERTS].set(moe_router[0])
    xf, hnp, gates, meta, cnt = _ssd_out_router(y, bf(ssd_w_o[0]), xf, row(norm_ffn[1]), router_p)

    n_blocks = -(-(t * TOP_K) // MOE_ROWS) + N_EXPERTS
    counts = cnt[0, :N_EXPERTS]
    padded = ((counts + MOE_ROWS - 1) // MOE_ROWS) * MOE_ROWS
    pad_ends = jnp.cumsum(padded)
    pad_starts = pad_ends - padded
    pos = pad_starts[meta[:, 0:TOP_K]] + meta[:, TOP_K:2 * TOP_K]
    n_used = (pad_ends[-1] // MOE_ROWS).astype(I32).reshape(1)
    block_expert = jnp.minimum(
        jnp.searchsorted(pad_ends, jnp.arange(n_blocks, dtype=I32) * MOE_ROWS, side='right'),
        N_EXPERTS - 1).astype(I32)
    bm_d = _blk(t, 256)
    pos_blocks = pos.astype(I32).reshape(t // bm_d, 1, TOP_K * bm_d)
    xb = _dispatch(hnp, pos_blocks, n_blocks * MOE_ROWS)
    yb = _moe(xb, block_expert, n_used, bf(moe_w_gate[0]), bf(moe_w_up[0]), bf(moe_w_down[0]), MOE_ROWS)
    out = _combine(xf, gates, row(norm_final), yb, pos_blocks)
    return out.reshape(bsz, s, d)
```

```python
import functools

import jax
import jax.numpy as jnp
from jax import lax
from jax.experimental import pallas as pl
from jax.experimental.pallas import tpu as pltpu

F32 = jnp.float32
BF16 = jnp.bfloat16
U32 = jnp.uint32
I32 = jnp.int32

RMS_EPS = 1e-6
MLA_HEADS = 16
MLA_Q_RANK = 512
MLA_KV_RANK = 512
MLA_NOPE = 128
MLA_ROPE = 64
MLA_V = 128
ROPE_THETA = 10000.0
SSD_HEADDIM = 64
SSD_GROUPS = 8
SSD_STATE = 128
SSD_CONV = 4
SSD_CHUNK = 256
N_EXPERTS = 8
TOP_K = 2
MOE_ROWS = 512

LANES = 128
QK_WIDTH = 2 * LANES
VMEM_LIMIT_BYTES = 56 * 1024 * 1024
HIGHEST = lax.Precision.HIGHEST


def _cparams(*sem):
    return pltpu.CompilerParams(dimension_semantics=sem, vmem_limit_bytes=VMEM_LIMIT_BYTES)


def _blk(n, pref):
    b = min(n, pref)
    assert n % b == 0, (n, pref)
    return b


def _resident(shape, index_map):
    return pl.BlockSpec(shape, index_map, pipeline_mode=pl.Buffered(1))


def _rms(xf, g):
    r = lax.rsqrt(jnp.mean(xf * xf, axis=-1, keepdims=True) + RMS_EPS)
    return xf * r * g


def _silu(x):
    return x * (1.0 / (1.0 + jnp.exp(-x)))


def _dot(a, b):
    return jnp.dot(a, b, preferred_element_type=F32)


def _dot_nt(a, b, precision=None):
    return lax.dot_general(a, b, (((1,), (1,)), ((), ())), preferred_element_type=F32, precision=precision)


def _dot_tn(a, b):
    return lax.dot_general(a, b, (((0,), (0,)), ((), ())), preferred_element_type=F32)


def _tile_lanes(x, reps):
    return x if reps == 1 else jnp.concatenate([x] * reps, axis=1)


def _split3(x):
    hi = x.astype(BF16).astype(F32)
    r1 = x - hi
    mid = r1.astype(BF16).astype(F32)
    lo = r1 - mid
    return jnp.concatenate([hi, mid, lo, jnp.zeros_like(x)], axis=0).astype(BF16)


def _mla_in_kernel(x_ref, g_ref, w_ref, qn_ref, kvn_ref, cr_ref, sr_ref, cq_ref, ckv_ref, kr_ref):
    hn = _rms(x_ref[...], g_ref[...]).astype(BF16)
    c = _dot(hn, w_ref[...])
    cq_ref[...] = _rms(c[:, :MLA_Q_RANK], qn_ref[...]).astype(BF16)
    ckv_ref[...] = _rms(c[:, MLA_Q_RANK:MLA_Q_RANK + MLA_KV_RANK], kvn_ref[...]).astype(BF16)
    o = MLA_Q_RANK + MLA_KV_RANK
    kr_ref[...] = (c[:, o:o + LANES] * cr_ref[...] + c[:, o + LANES:o + 2 * LANES] * sr_ref[...]).astype(BF16)


def _mla_in(x, g, w_in_p, q_norm, kv_norm, cr, sr):
    t, d = x.shape
    bm = _blk(t, 512)
    wn = w_in_p.shape[1]
    row = lambda i: (i, 0)
    fixed = lambda i: (0, 0)
    return pl.pallas_call(
        _mla_in_kernel,
        grid=(t // bm,),
        in_specs=[
            pl.BlockSpec((bm, d), row),
            _resident((1, d), fixed),
            _resident((d, wn), fixed),
            _resident((1, MLA_Q_RANK), fixed),
            _resident((1, MLA_KV_RANK), fixed),
            pl.BlockSpec((bm, LANES), row),
            pl.BlockSpec((bm, LANES), row),
        ],
        out_specs=[
            pl.BlockSpec((bm, MLA_Q_RANK), row),
            pl.BlockSpec((bm, MLA_KV_RANK), row),
            pl.BlockSpec((bm, LANES), row),
        ],
        out_shape=[
            jax.ShapeDtypeStruct((t, MLA_Q_RANK), BF16),
            jax.ShapeDtypeStruct((t, MLA_KV_RANK), BF16),
            jax.ShapeDtypeStruct((t, LANES), BF16),
        ],
        compiler_params=_cparams("parallel"),
        name="mla_in",
    )(x, g, w_in_p, q_norm, kv_norm, cr, sr)


def _mla_up_kernel(cq_ref, ckv_ref, kr_ref, wq_ref, wqs_ref, wk_ref, wv_ref, cr_ref, sr_ref,
                   q_ref, k_ref, v_ref, *, scale):
    cq = cq_ref[...]
    ckv = ckv_ref[...]
    a = _dot(cq, wq_ref[...])
    b = _dot(cq, wqs_ref[...])
    kn = _dot(ckv, wk_ref[...])
    v_ref[...] = _dot(ckv, wv_ref[...]).astype(BF16)
    cr = cr_ref[...] * scale
    sr = sr_ref[...] * scale
    kr = kr_ref[...]
    for h in range(MLA_HEADS):
        q0 = h * QK_WIDTH
        q_ref[:, q0:q0 + LANES] = (a[:, q0:q0 + LANES] * scale).astype(BF16)
        q_ref[:, q0 + LANES:q0 + QK_WIDTH] = (
            a[:, q0 + LANES:q0 + QK_WIDTH] * cr + b[:, h * LANES:(h + 1) * LANES] * sr).astype(BF16)
        k_ref[:, q0:q0 + LANES] = kn[:, h * LANES:(h + 1) * LANES].astype(BF16)
        k_ref[:, q0 + LANES:q0 + QK_WIDTH] = kr


def _mla_up(cq, ckv, kr, wq, wqs, wk, wv, cr, sr):
    t = cq.shape[0]
    bm = _blk(t, 256)
    row = lambda i: (i, 0)
    fixed = lambda i: (0, 0)
    hq = MLA_HEADS * QK_WIDTH
    hv = MLA_HEADS * MLA_V
    scale = float((MLA_NOPE + MLA_ROPE) ** -0.5) * 1.4426950408889634
    return pl.pallas_call(
        functools.partial(_mla_up_kernel, scale=scale),
        grid=(t // bm,),
        in_specs=[
            pl.BlockSpec((bm, MLA_Q_RANK), row),
            pl.BlockSpec((bm, MLA_KV_RANK), row),
            pl.BlockSpec((bm, LANES), row),
            _resident(wq.shape, fixed),
            _resident(wqs.shape, fixed),
            _resident(wk.shape, fixed),
            _resident(wv.shape, fixed),
            pl.BlockSpec((bm, LANES), row),
            pl.BlockSpec((bm, LANES), row),
        ],
        out_specs=[
            pl.BlockSpec((bm, hq), row),
            pl.BlockSpec((bm, hq), row),
            pl.BlockSpec((bm, hv), row),
        ],
        out_shape=[
            jax.ShapeDtypeStruct((t, hq), BF16),
            jax.ShapeDtypeStruct((t, hq), BF16),
            jax.ShapeDtypeStruct((t, hv), BF16),
        ],
        compiler_params=_cparams("parallel"),
        name="mla_up",
    )(cq, ckv, kr, wq, wqs, wk, wv, cr, sr)


def _flash_kernel(q_ref, k_ref, v_ref, o_ref, m_scr, l_scr, acc_scr, *, tq, tk, rows):
    i = pl.program_id(2)
    m_scr[...] = jnp.full(m_scr.shape, -jnp.inf, F32)
    l_scr[...] = jnp.zeros(l_scr.shape, F32)
    acc_scr[...] = jnp.zeros(acc_scr.shape, F32)

    def update(r, k0, width, masked):
        rs = slice(r * rows, (r + 1) * rows)
        s = _dot_nt(q_ref[0, rs, :], k_ref[0, pl.ds(k0, width), :])
        if masked:
            qrel = r * rows + lax.broadcasted_iota(I32, (rows, width), 0)
            krel = lax.broadcasted_iota(I32, (rows, width), 1)
            s = jnp.where(krel <= qrel, s, -jnp.inf)
        m_prev = m_scr[rs, :]
        m_new = jnp.maximum(m_prev, jnp.max(s, axis=1, keepdims=True))
        alpha = jnp.exp2(m_prev - m_new)
        p = jnp.exp2(s - _tile_lanes(m_new, width // LANES))
        l_scr[rs, :] = alpha * l_scr[rs, :] + jnp.sum(p, axis=1, keepdims=True)
        acc_scr[rs, :] = alpha * acc_scr[rs, :] + _dot(p.astype(BF16), v_ref[0, pl.ds(k0, width), :])
        m_scr[rs, :] = m_new

    def body(j, carry):
        for r in range(tq // rows):
            update(r, pl.multiple_of(j * tk, tk), tk, False)
        return carry

    lax.fori_loop(0, i * (tq // tk), body, 0)
    for r in range(tq // rows):
        update(r, pl.multiple_of(i * tq, tq), (r + 1) * rows, True)
    o_ref[0] = (acc_scr[...] / l_scr[...]).astype(o_ref.dtype)


def _flash(q, k, v, bsz, s):
    assert MLA_V == LANES
    tq = _blk(s, 1024)
    tk = _blk(tq, 512)
    rows = _blk(tq, 256)
    return pl.pallas_call(
        functools.partial(_flash_kernel, tq=tq, tk=tk, rows=rows),
        grid=(bsz, MLA_HEADS, s // tq),
        in_specs=[
            pl.BlockSpec((1, tq, QK_WIDTH), lambda b, h, i: (b, i, h)),
            pl.BlockSpec((1, s, QK_WIDTH), lambda b, h, i: (b, 0, h)),
            pl.BlockSpec((1, s, MLA_V), lambda b, h, i: (b, 0, h)),
        ],
        out_specs=pl.BlockSpec((1, tq, MLA_V), lambda b, h, i: (b, i, h)),
        out_shape=jax.ShapeDtypeStruct((bsz, s, MLA_HEADS * MLA_V), BF16),
        scratch_shapes=[
            pltpu.VMEM((tq, LANES), F32),
            pltpu.VMEM((tq, LANES), F32),
            pltpu.VMEM((tq, MLA_V), F32),
        ],
        compiler_params=_cparams("parallel", "parallel", "parallel"),
        name="mla_flash",
    )(q, k, v)


def _proj_res_norm_kernel(a_ref, w_ref, x_ref, g_ref, xo_ref, hn_ref):
    k = pl.program_id(1)

    @pl.when(k == 0)
    def _():
        xo_ref[...] = x_ref[...]

    xo_ref[...] += _dot(a_ref[...], w_ref[...])

    @pl.when(k == pl.num_programs(1) - 1)
    def _():
        hn_ref[...] = _rms(xo_ref[...], g_ref[...]).astype(hn_ref.dtype)


def _proj_res_norm(a, w, x, g):
    t, kdim = a.shape
    d = w.shape[1]
    bm = _blk(t, 512)
    bk = _blk(kdim, 1024)
    return pl.pallas_call(
        _proj_res_norm_kernel,
        grid=(t // bm, kdim // bk),
        in_specs=[
            pl.BlockSpec((bm, bk), lambda i, k: (i, k)),
            pl.BlockSpec((bk, d), lambda i, k: (k, 0)),
            pl.BlockSpec((bm, d), lambda i, k: (i, 0)),
            _resident((1, d), lambda i, k: (0, 0)),
        ],
        out_specs=[
            pl.BlockSpec((bm, d), lambda i, k: (i, 0)),
            pl.BlockSpec((bm, d), lambda i, k: (i, 0)),
        ],
        out_shape=[
            jax.ShapeDtypeStruct((t, d), F32),
            jax.ShapeDtypeStruct((t, d), BF16),
        ],
        compiler_params=_cparams("parallel", "arbitrary"),
        name="proj_res_norm",
    )(a, w, x, g)


def _ffn_kernel(hn_ref, x_ref, wg_ref, wu_ref, wd_ref, o_ref):
    f = pl.program_id(1)

    @pl.when(f == 0)
    def _():
        o_ref[...] = x_ref[...]

    hn = hn_ref[...]
    h = (_silu(_dot(hn, wg_ref[...])) * _dot(hn, wu_ref[...])).astype(BF16)
    o_ref[...] += _dot(h, wd_ref[...])


def _ffn(hn, x, wg, wu, wd):
    t, d = x.shape
    fdim = wg.shape[1]
    bm = _blk(t, 512)
    bf = _blk(fdim, 512)
    return pl.pallas_call(
        _ffn_kernel,
        grid=(t // bm, fdim // bf),
        in_specs=[
            pl.BlockSpec((bm, d), lambda i, f: (i, 0)),
            pl.BlockSpec((bm, d), lambda i, f: (i, 0)),
            pl.BlockSpec((d, bf), lambda i, f: (0, f)),
            pl.BlockSpec((d, bf), lambda i, f: (0, f)),
            pl.BlockSpec((bf, d), lambda i, f: (f, 0)),
        ],
        out_specs=pl.BlockSpec((bm, d), lambda i, f: (i, 0)),
        out_shape=jax.ShapeDtypeStruct((t, d), F32),
        compiler_params=_cparams("parallel", "arbitrary"),
        name="dense_ffn",
    )(hn, x, wg, wu, wd)


def _ssd_in_kernel(x_ref, g_ref, w_ref, wdt_ref, dtb_ref, zx_ref, dtt_ref, hn_scr):
    j = pl.program_id(1)

    @pl.when(j == 0)
    def _():
        hn = _rms(x_ref[...], g_ref[...]).astype(BF16)
        hn_scr[...] = hn
        raw = _dot_nt(wdt_ref[...], hn) + dtb_ref[...]
        dtt_ref[...] = jnp.maximum(raw, 0.0) + jnp.log1p(jnp.exp(-jnp.abs(raw)))

    zx_ref[...] = _dot(hn_scr[...], w_ref[...]).astype(zx_ref.dtype)


def _ssd_in(x, g, w_main, w_dt_t, dt_bias_col):
    t, d = x.shape
    n = w_main.shape[1]
    bm = _blk(t, 1024)
    bn = _blk(n, 1024)
    hp = w_dt_t.shape[0]
    return pl.pallas_call(
        _ssd_in_kernel,
        grid=(t // bm, n // bn),
        in_specs=[
            pl.BlockSpec((bm, d), lambda i, j: (i, 0)),
            _resident((1, d), lambda i, j: (0, 0)),
            pl.BlockSpec((d, bn), lambda i, j: (0, j)),
            _resident((hp, d), lambda i, j: (0, 0)),
            _resident((hp, 1), lambda i, j: (0, 0)),
        ],
        out_specs=[
            pl.BlockSpec((bm, bn), lambda i, j: (i, j)),
            pl.BlockSpec((hp, bm), lambda i, j: (0, i)),
        ],
        out_shape=[
            jax.ShapeDtypeStruct((t, n), BF16),
            jax.ShapeDtypeStruct((hp, t), F32),
        ],
        scratch_shapes=[pltpu.VMEM((bm, d), BF16)],
        compiler_params=_cparams("parallel", "arbitrary"),
        name="ssd_in",
    )(x, g, w_main, w_dt_t, dt_bias_col)


def _ssd_kernel(z_ref, x_ref, b_ref, c_ref, dtt_ref, alog_ref, cwx_ref, cwb_ref, cwc_ref,
                cbx_ref, cbb_ref, cbc_ref, dsk_ref, ng_ref, y_ref, ubuf, state_scr, *, L, gw):
    c_idx = pl.program_id(2)
    hg = gw // SSD_HEADDIM
    n = SSD_STATE
    tail = 8

    @pl.when(c_idx == 0)
    def _():
        ubuf[0:tail, :] = jnp.zeros((tail, ubuf.shape[1]), F32)
        state_scr[...] = jnp.zeros(state_scr.shape, F32)

    ubuf[tail:tail + L, 0:gw] = x_ref[...].astype(F32)
    ubuf[tail:tail + L, gw:gw + n] = b_ref[...].astype(F32)
    ubuf[tail:tail + L, gw + n:gw + 2 * n] = c_ref[...].astype(F32)

    cw = jnp.concatenate([cwx_ref[...], cwb_ref[...], cwc_ref[...]], axis=1)
    cb = jnp.concatenate([cbx_ref[...], cbb_ref[...], cbc_ref[...]], axis=1)
    conv = cb + cw[SSD_CONV - 1:SSD_CONV, :] * ubuf[tail:tail + L, :]
    for kk in range(SSD_CONV - 1):
        off = tail - (SSD_CONV - 1) + kk
        conv = conv + cw[kk:kk + 1, :] * ubuf[off:off + L, :]
    ubuf[0:tail, :] = ubuf[L:L + tail, :]
    u = _silu(conv)
    xc = u[:, 0:gw]
    bc = u[:, gw:gw + n].astype(BF16)
    cc = u[:, gw + n:gw + 2 * n].astype(BF16)

    dtt = dtt_ref[...]
    a_t = dtt * (-jnp.exp(alog_ref[...]))
    ri = lax.broadcasted_iota(I32, (L, L), 0)
    ci = lax.broadcasted_iota(I32, (L, L), 1)
    causal = ri >= ci
    tril = jnp.where(causal, 1.0, 0.0).astype(BF16)
    r3 = _dot_nt(_split3(a_t), tril)
    acum_row = r3[0:hg] + r3[hg:2 * hg] + r3[2 * hg:3 * hg]

    def expansion(width):
        hrow = lax.broadcasted_iota(I32, (4 * hg, hg * width), 0)
        hlane = lax.broadcasted_iota(I32, (4 * hg, hg * width), 1) // width
        return jnp.where((hrow % hg == hlane) & (hrow < 3 * hg), 1.0, 0.0).astype(BF16)

    acum3 = _split3(acum_row)
    dt_e = _dot_tn(_split3(dtt), expansion(SSD_HEADDIM))
    acum_e = _dot_tn(acum3, expansion(SSD_HEADDIM))
    acum_c = _dot_tn(acum3, expansion(LANES))
    last_e = acum_e[L - 1:L, :]

    xdt = xc * dt_e
    cbm = _dot_nt(cc, bc)

    st = state_scr[...]
    y = _dot(cc, st.astype(BF16)) * jnp.exp(acum_e) + xc * dsk_ref[...]

    lane = lax.broadcasted_iota(I32, (L, LANES), 1)
    pieces = []
    for pair in range(hg // 2):
        xp = xdt[:, pair * LANES:(pair + 1) * LANES]
        yp = None
        for sub in range(2):
            jh = 2 * pair + sub
            col = _tile_lanes(acum_c[:, jh * LANES:(jh + 1) * LANES], L // LANES)
            seg = col - acum_row[jh:jh + 1, :]
            m = (cbm * jnp.exp(jnp.where(causal, seg, -jnp.inf))).astype(BF16)
            keep = (lane // SSD_HEADDIM) == sub
            d = _dot(m, jnp.where(keep, xp, 0.0).astype(BF16))
            yp = d if yp is None else yp + d
        pieces.append(yp)
    y = y + jnp.concatenate(pieces, axis=1)

    xw = (xdt * jnp.exp(last_e - acum_e)).astype(BF16)
    state_scr[...] = st * jnp.exp(last_e) + _dot_tn(bc, xw)

    zf = z_ref[...].astype(F32)
    yg = y * _silu(zf)
    yg = yg * lax.rsqrt(jnp.mean(yg * yg, axis=-1, keepdims=True) + RMS_EPS)
    y_ref[...] = (yg * ng_ref[...]).astype(y_ref.dtype)


def _ssd_core(zx, dtt, a_log_col, conv_w, conv_b, d_exp, norm_g, bsz, s, inner):
    t = zx.shape[0]
    L = SSD_CHUNK
    assert s % L == 0
    nc = s // L
    gw = inner // SSD_GROUPS
    hg = gw // SSD_HEADDIM
    n = SSD_STATE
    gn = SSD_GROUPS * n
    xb0 = inner // gw
    bb0 = (2 * inner) // n
    cb0 = (2 * inner + gn) // n
    cwb0 = inner // n
    cwc0 = (inner + gn) // n
    rowi = lambda b, g, c: b * nc + c
    return pl.pallas_call(
        functools.partial(_ssd_kernel, L=L, gw=gw),
        grid=(bsz, SSD_GROUPS, nc),
        in_specs=[
            pl.BlockSpec((L, gw), lambda b, g, c: (rowi(b, g, c), g)),
            pl.BlockSpec((L, gw), lambda b, g, c: (rowi(b, g, c), xb0 + g)),
            pl.BlockSpec((L, n), lambda b, g, c: (rowi(b, g, c), bb0 + g)),
            pl.BlockSpec((L, n), lambda b, g, c: (rowi(b, g, c), cb0 + g)),
            pl.BlockSpec((hg, L), lambda b, g, c: (g, rowi(b, g, c))),
            pl.BlockSpec((hg, 1), lambda b, g, c: (g, 0)),
            pl.BlockSpec((SSD_CONV, gw), lambda b, g, c: (0, g)),
            pl.BlockSpec((SSD_CONV, n), lambda b, g, c: (0, cwb0 + g)),
            pl.BlockSpec((SSD_CONV, n), lambda b, g, c: (0, cwc0 + g)),
            pl.BlockSpec((1, gw), lambda b, g, c: (0, g)),
            pl.BlockSpec((1, n), lambda b, g, c: (0, cwb0 + g)),
            pl.BlockSpec((1, n), lambda b, g, c: (0, cwc0 + g)),
            pl.BlockSpec((1, gw), lambda b, g, c: (0, g)),
            pl.BlockSpec((1, gw), lambda b, g, c: (0, g)),
        ],
        out_specs=pl.BlockSpec((L, gw), lambda b, g, c: (rowi(b, g, c), g)),
        out_shape=jax.ShapeDtypeStruct((t, inner), BF16),
        scratch_shapes=[
            pltpu.VMEM((L + 8, gw + 2 * n), F32),
            pltpu.VMEM((n, gw), F32),
        ],
        compiler_params=_cparams("parallel", "parallel", "arbitrary"),
        name="ssd_core",
    )(zx, zx, zx, zx, dtt, a_log_col, conv_w, conv_w, conv_w, conv_b, conv_b, conv_b, d_exp, norm_g)


def _ssd_out_router_kernel(a_ref, w_ref, x_ref, g_ref, r_ref, xo_ref, hnp_ref, gate_ref, meta_ref, cnt_ref,
                           run_scr):
    i = pl.program_id(0)
    k = pl.program_id(1)
    bm, d = xo_ref.shape

    @pl.when((i == 0) & (k == 0))
    def _():
        run_scr[...] = jnp.zeros(run_scr.shape, F32)

    @pl.when(k == 0)
    def _():
        xo_ref[...] = x_ref[...]

    xo_ref[...] += _dot(a_ref[...], w_ref[...])

    @pl.when(k == pl.num_programs(1) - 1)
    def _():
        hn = _rms(xo_ref[...], g_ref[...])
        hn_hi = hn.astype(BF16)
        bits = pltpu.bitcast(hn_hi.astype(F32), U32)
        hnp_ref[...] = (bits[:, d // 2:] & jnp.uint32(0xFFFF0000)) | (bits[:, :d // 2] >> 16)

        hn_lo = (hn - hn_hi.astype(F32)).astype(BF16)
        rt = r_ref[...]
        rt_hi = rt.astype(BF16)
        rt_lo = (rt - rt_hi.astype(F32)).astype(BF16)
        logits = _dot(hn_hi, rt_hi) + (_dot(hn_lo, rt_hi) + _dot(hn_hi, rt_lo))
        lane = lax.broadcasted_iota(I32, logits.shape, 1)
        logits = jnp.where(lane < N_EXPERTS, logits, -jnp.inf)
        m1 = jnp.max(logits, axis=1, keepdims=True)
        i1 = jnp.min(jnp.where(logits == m1, lane, LANES), axis=1, keepdims=True)
        rest = jnp.where(lane == i1, -jnp.inf, logits)
        m2 = jnp.max(rest, axis=1, keepdims=True)
        i2 = jnp.min(jnp.where(rest == m2, lane, LANES), axis=1, keepdims=True)
        e21 = jnp.exp(m2 - m1)
        g1 = 1.0 / (1.0 + e21)
        g2 = e21 / (1.0 + e21)
        gate_ref[...] = jnp.where(lane == 0, g1, jnp.where(lane == 1, g2, 0.0))

        oh1 = lane == i1
        oh2 = lane == i2
        both = (oh1 | oh2).astype(BF16)
        ri = lax.broadcasted_iota(I32, (bm, bm), 0)
        ci = lax.broadcasted_iota(I32, (bm, bm), 1)
        before = _dot((ri > ci).astype(BF16), both) + run_scr[...]
        r1 = jnp.sum(jnp.where(oh1, before, 0.0), axis=1, keepdims=True)
        r2 = jnp.sum(jnp.where(oh2, before, 0.0), axis=1, keepdims=True)
        meta = jnp.where(lane == 0, i1.astype(F32),
                         jnp.where(lane == 1, i2.astype(F32),
                                   jnp.where(lane == 2, r1, jnp.where(lane == 3, r2, 0.0))))
        meta_ref[...] = meta.astype(I32)
        run_scr[...] = run_scr[...] + jnp.sum(both.astype(F32), axis=0, keepdims=True)
        cnt_ref[...] = jnp.broadcast_to(run_scr[...], cnt_ref.shape).astype(I32)


def _ssd_out_router(a, w, x, g, router_p):
    t, kdim = a.shape
    d = w.shape[1]
    bm = _blk(t, 512)
    bk = _blk(kdim, 1024)
    row = lambda i, k: (i, 0)
    return pl.pallas_call(
        _ssd_out_router_kernel,
        grid=(t // bm, kdim // bk),
        in_specs=[
            pl.BlockSpec((bm, bk), lambda i, k: (i, k)),
            pl.BlockSpec((bk, d), lambda i, k: (k, 0)),
            pl.BlockSpec((bm, d), row),
            _resident((1, d), lambda i, k: (0, 0)),
            _resident((d, LANES), lambda i, k: (0, 0)),
        ],
        out_specs=[
            pl.BlockSpec((bm, d), row),
            pl.BlockSpec((bm, d // 2), row),
            pl.BlockSpec((bm, LANES), row),
            pl.BlockSpec((bm, LANES), row),
            pl.BlockSpec((8, LANES), lambda i, k: (0, 0)),
        ],
        out_shape=[
            jax.ShapeDtypeStruct((t, d), F32),
            jax.ShapeDtypeStruct((t, d // 2), U32),
            jax.ShapeDtypeStruct((t, LANES), F32),
            jax.ShapeDtypeStruct((t, LANES), I32),
            jax.ShapeDtypeStruct((8, LANES), I32),
        ],
        scratch_shapes=[pltpu.VMEM((1, LANES), F32)],
        compiler_params=_cparams("arbitrary", "arbitrary"),
        name="ssd_out_router",
    )(a, w, x, g, router_p)


def _dispatch_kernel(pos_ref, hn_ref, xb_in_ref, xb_ref, sem):
    del xb_in_ref
    bm = hn_ref.shape[0]

    def row_copy(r, kk):
        dst = pos_ref[0, 0, TOP_K * r + kk]
        return pltpu.make_async_copy(hn_ref.at[pl.ds(r, 1)], xb_ref.at[pl.ds(dst, 1)], sem)

    def start(r, carry):
        for kk in range(TOP_K):
            row_copy(r, kk).start()
        return carry

    def wait(r, carry):
        for kk in range(TOP_K):
            row_copy(r, kk).wait()
        return carry

    lax.fori_loop(0, bm, start, 0)
    lax.fori_loop(0, bm, wait, 0)


def _dispatch(hnp, pos_blocks, cap):
    t, w = hnp.shape
    nb, _, two_bm = pos_blocks.shape
    bm = two_bm // TOP_K
    xb_init = jnp.zeros((cap, w), U32)
    return pl.pallas_call(
        _dispatch_kernel,
        grid=(nb,),
        in_specs=[
            pl.BlockSpec((1, 1, two_bm), lambda i: (i, 0, 0), memory_space=pltpu.SMEM),
            pl.BlockSpec((bm, w), lambda i: (i, 0)),
            pl.BlockSpec(memory_space=pl.ANY),
        ],
        out_specs=pl.BlockSpec(memory_space=pl.ANY),
        out_shape=jax.ShapeDtypeStruct((cap, w), U32),
        scratch_shapes=[pltpu.SemaphoreType.DMA(())],
        input_output_aliases={2: 0},
        compiler_params=_cparams("arbitrary"),
        name="moe_dispatch",
    )(pos_blocks, hnp, xb_init)


def _moe_kernel(be_ref, nu_ref, xb_ref, wg_ref, wu_ref, wd_ref, o_ref, xs_scr):
    del be_ref
    b = pl.program_id(0)
    f = pl.program_id(1)

    @pl.when(f == 0)
    def _():
        o_ref[...] = jnp.zeros(o_ref.shape, F32)

    @pl.when(b < nu_ref[0])
    def _():
        @pl.when(f == 0)
        def _():
            w = xb_ref[...]
            half = w.shape[1]
            xs_scr[:, :half] = pltpu.bitcast(w << 16, F32).astype(BF16)
            xs_scr[:, half:] = pltpu.bitcast(w & jnp.uint32(0xFFFF0000), F32).astype(BF16)

        xs = xs_scr[...]
        h = (_silu(_dot(xs, wg_ref[0])) * _dot(xs, wu_ref[0])).astype(BF16)
        o_ref[...] += _dot(h, wd_ref[0])


def _moe(xb, block_expert, n_used, wg, wu, wd, bm):
    cap, half = xb.shape
    d = 2 * half
    fdim = wg.shape[2]
    bf = _blk(fdim, 512)
    nf = fdim // bf
    nb = cap // bm

    def live(b, nu):
        return jnp.minimum(b, nu[0] - 1)

    def f_eff(b, f, nu):
        return jnp.where(b < nu[0], f, nf - 1)

    grid_spec = pltpu.PrefetchScalarGridSpec(
        num_scalar_prefetch=2,
        grid=(nb, nf),
        in_specs=[
            pl.BlockSpec((bm, half), lambda b, f, be, nu: (live(b, nu), 0)),
            pl.BlockSpec((1, d, bf), lambda b, f, be, nu: (be[live(b, nu)], 0, f_eff(b, f, nu))),
            pl.BlockSpec((1, d, bf), lambda b, f, be, nu: (be[live(b, nu)], 0, f_eff(b, f, nu))),
            pl.BlockSpec((1, bf, d), lambda b, f, be, nu: (be[live(b, nu)], f_eff(b, f, nu), 0)),
        ],
        out_specs=pl.BlockSpec((bm, d), lambda b, f, be, nu: (b, 0)),
        scratch_shapes=[pltpu.VMEM((bm, d), BF16)],
    )
    return pl.pallas_call(
        _moe_kernel,
        grid_spec=grid_spec,
        out_shape=jax.ShapeDtypeStruct((cap, d), F32),
        compiler_params=_cparams("arbitrary", "arbitrary"),
        name="moe_ffn",
    )(block_expert, n_used, xb, wg, wu, wd)


def _combine_kernel(pos_ref, x_ref, gate_ref, g_ref, yb_ref, o_ref, ybuf, sem):
    bm = x_ref.shape[0]

    def row_copy(r, kk):
        src = pos_ref[0, 0, TOP_K * r + kk]
        return pltpu.make_async_copy(yb_ref.at[pl.ds(src, 1)], ybuf.at[kk, pl.ds(r, 1)], sem)

    def start(r, carry):
        for kk in range(TOP_K):
            row_copy(r, kk).start()
        return carry

    def wait(r, carry):
        for kk in range(TOP_K):
            row_copy(r, kk).wait()
        return carry

    lax.fori_loop(0, bm, start, 0)
    lax.fori_loop(0, bm, wait, 0)
    gates = gate_ref[...]
    x = x_ref[...] + gates[:, 0:1] * ybuf[0] + gates[:, 1:2] * ybuf[1]
    o_ref[...] = _rms(x, g_ref[...])


def _combine(x, gates, g_final, yb, pos_blocks):
    t, d = x.shape
    nb, _, two_bm = pos_blocks.shape
    bm = two_bm // TOP_K
    return pl.pallas_call(
        _combine_kernel,
        grid=(nb,),
        in_specs=[
            pl.BlockSpec((1, 1, two_bm), lambda i: (i, 0, 0), memory_space=pltpu.SMEM),
            pl.BlockSpec((bm, d), lambda i: (i, 0)),
            pl.BlockSpec((bm, LANES), lambda i: (i, 0)),
            _resident((1, d), lambda i: (0, 0)),
            pl.BlockSpec(memory_space=pl.ANY),
        ],
        out_specs=pl.BlockSpec((bm, d), lambda i: (i, 0)),
        out_shape=jax.ShapeDtypeStruct((t, d), F32),
        scratch_shapes=[pltpu.VMEM((TOP_K, bm, d), F32), pltpu.SemaphoreType.DMA(())],
        compiler_params=_cparams("arbitrary"),
        name="moe_combine",
    )(pos_blocks, x, gates, g_final, yb)


def _rope_tables(positions):
    inv_freq = ROPE_THETA ** (-jnp.arange(0, MLA_ROPE, 2, dtype=F32) / MLA_ROPE)
    ang = positions.astype(F32).reshape(-1)[:, None] * inv_freq
    cos, sin = jnp.cos(ang), jnp.sin(ang)
    zeros = jnp.zeros((ang.shape[0], LANES - MLA_ROPE), F32)
    return jnp.concatenate([cos, cos, zeros], axis=1), jnp.concatenate([-sin, sin, zeros], axis=1)


def _swap_halves(w):
    half = w.shape[-1] // 2
    return jnp.concatenate([w[..., half:], w[..., :half]], axis=-1)


def _mla_weights(w_in, w_uq, w_ukv):
    d = w_in.shape[0]
    o = MLA_Q_RANK + MLA_KV_RANK
    w_kr = w_in[:, o:]
    zpad = jnp.zeros((d, LANES - MLA_ROPE), w_in.dtype)
    w_in_p = jnp.concatenate([w_in[:, :o], w_kr, zpad, _swap_halves(w_kr), zpad], axis=1).astype(BF16)
    uq = w_uq.reshape(MLA_Q_RANK, MLA_HEADS, MLA_NOPE + MLA_ROPE)
    rope = uq[:, :, MLA_NOPE:]
    zq = jnp.zeros((MLA_Q_RANK, MLA_HEADS, LANES - MLA_ROPE), w_uq.dtype)
    wq = jnp.concatenate([uq[:, :, :MLA_NOPE], rope, zq], axis=2).reshape(MLA_Q_RANK, -1).astype(BF16)
    wqs = jnp.concatenate([_swap_halves(rope), zq], axis=2).reshape(MLA_Q_RANK, -1).astype(BF16)
    ukv = w_ukv.reshape(MLA_KV_RANK, MLA_HEADS, MLA_NOPE + MLA_V)
    wk = ukv[:, :, :MLA_NOPE].reshape(MLA_KV_RANK, -1).astype(BF16)
    wv = ukv[:, :, MLA_NOPE:].reshape(MLA_KV_RANK, -1).astype(BF16)
    return w_in_p, wq, wqs, wk, wv


def kernel(x, positions, mla_w_in, mla_q_norm, mla_kv_norm, mla_w_uq, mla_w_ukv, mla_w_o, ssd_w_in, ssd_conv_w, ssd_conv_b, ssd_dt_bias, ssd_a_log, ssd_d, ssd_norm, ssd_w_o, ffn_w_gate, ffn_w_up, ffn_w_down, moe_router, moe_w_gate, moe_w_up, moe_w_down, norm_mix, norm_ffn, norm_final):
    bsz, s, d = x.shape
    t = bsz * s
    assert norm_mix.shape[0] == 2 and mla_w_in.shape[0] == 1 and ssd_w_in.shape[0] == 1
    xf = x.reshape(t, d)
    cr, sr = _rope_tables(positions)
    row = lambda v: v.reshape(1, -1)
    bf = lambda v: v.astype(BF16)

    w_in_p, wq, wqs, wk, wv = _mla_weights(mla_w_in[0], mla_w_uq[0], mla_w_ukv[0])
    cq, ckv, kr = _mla_in(xf, row(norm_mix[0]), w_in_p, row(mla_q_norm[0]), row(mla_kv_norm[0]), cr, sr)
    q, k, v = _mla_up(cq, ckv, kr, wq, wqs, wk, wv, cr, sr)
    o = _flash(q.reshape(bsz, s, -1), k.reshape(bsz, s, -1), v.reshape(bsz, s, -1), bsz, s)
    xf, hn = _proj_res_norm(o.reshape(t, -1), bf(mla_w_o[0]), xf, row(norm_ffn[0]))
    xf = _ffn(hn, xf, bf(ffn_w_gate[0]), bf(ffn_w_up[0]), bf(ffn_w_down[0]))

    inner = ssd_norm.shape[1]
    heads = inner // SSD_HEADDIM
    n_main = 2 * inner + 2 * SSD_GROUPS * SSD_STATE
    w_in = ssd_w_in[0]
    w_dt_t = bf(jnp.zeros((LANES, d), F32).at[:heads].set(w_in[:, n_main:].T))
    dtb = jnp.zeros((LANES, 1), F32).at[:heads, 0].set(ssd_dt_bias[0])
    zx, dtt = _ssd_in(xf, row(norm_mix[1]), bf(w_in[:, :n_main]), w_dt_t, dtb)
    a_log_col = jnp.zeros((LANES, 1), F32).at[:heads, 0].set(ssd_a_log[0])
    d_exp = jnp.repeat(ssd_d[0], SSD_HEADDIM).reshape(1, inner)
    y = _ssd_core(zx, dtt, a_log_col, ssd_conv_w[0], row(ssd_conv_b[0]), d_exp, row(ssd_norm[0]), bsz, s, inner)
    router_p = jnp.zeros((d, LANES), F32).at[:, :N_EXPERTS].set(moe_router[0])
    xf, hnp, gates, meta, cnt = _ssd_out_router(y, bf(ssd_w_o[0]), xf, row(norm_ffn[1]), router_p)

    n_blocks = -(-(t * TOP_K) // MOE_ROWS) + N_EXPERTS
    counts = cnt[0, :N_EXPERTS]
    padded = ((counts + MOE_ROWS - 1) // MOE_ROWS) * MOE_ROWS
    pad_ends = jnp.cumsum(padded)
    pad_starts = pad_ends - padded
    pos = pad_starts[meta[:, 0:TOP_K]] + meta[:, TOP_K:2 * TOP_K]
    n_used = (pad_ends[-1] // MOE_ROWS).astype(I32).reshape(1)
    block_expert = jnp.minimum(
        jnp.searchsorted(pad_ends, jnp.arange(n_blocks, dtype=I32) * MOE_ROWS, side='right'),
        N_EXPERTS - 1).astype(I32)
    bm_d = _blk(t, 256)
    pos_blocks = pos.astype(I32).reshape(t // bm_d, 1, TOP_K * bm_d)
    xb = _dispatch(hnp, pos_blocks, n_blocks * MOE_ROWS)
    yb = _moe(xb, block_expert, n_used, bf(moe_w_gate[0]), bf(moe_w_up[0]), bf(moe_w_down[0]), MOE_ROWS)
    out = _combine(xf, gates, row(norm_final), yb, pos_blocks)
    return out.reshape(bsz, s, d)
```

```python
import functools

import jax
import jax.numpy as jnp
from jax import lax
from jax.experimental import pallas as pl
from jax.experimental.pallas import tpu as pltpu

F32 = jnp.float32
BF16 = jnp.bfloat16
U32 = jnp.uint32
I32 = jnp.int32

RMS_EPS = 1e-6
MLA_HEADS = 16
MLA_Q_RANK = 512
MLA_KV_RANK = 512
MLA_NOPE = 128
MLA_ROPE = 64
MLA_V = 128
ROPE_THETA = 10000.0
SSD_HEADDIM = 64
SSD_GROUPS = 8
SSD_STATE = 128
SSD_CONV = 4
SSD_CHUNK = 256
N_EXPERTS = 8
TOP_K = 2
MOE_ROWS = 512

LANES = 128
BF16_SUBLANES = 16
QK_WIDTH = 2 * LANES
VMEM_LIMIT_BYTES = 56 * 1024 * 1024
LOG2_E = 1.4426950408889634


def _cparams(*sem):
    return pltpu.CompilerParams(dimension_semantics=sem, vmem_limit_bytes=VMEM_LIMIT_BYTES)


def _blk(n, pref):
    b = min(n, pref)
    assert n % b == 0, (n, pref)
    return b


def _resident(shape, index_map):
    return pl.BlockSpec(shape, index_map, pipeline_mode=pl.Buffered(1))


def _rms(xf, g):
    r = lax.rsqrt(jnp.mean(xf * xf, axis=-1, keepdims=True) + RMS_EPS)
    return xf * r * g


def _silu(x):
    h = 0.5 * x
    return h + h * jnp.tanh(h)


def _dot(a, b):
    return jnp.dot(a, b, preferred_element_type=F32)


def _dot_nt(a, b, precision=None):
    return lax.dot_general(a, b, (((1,), (1,)), ((), ())), preferred_element_type=F32, precision=precision)


def _dot_tn(a, b):
    return lax.dot_general(a, b, (((0,), (0,)), ((), ())), preferred_element_type=F32)


def _tile_lanes(x, reps):
    return x if reps == 1 else jnp.concatenate([x] * reps, axis=1)


def _split3(x):
    hi = x.astype(BF16).astype(F32)
    r1 = x - hi
    mid = r1.astype(BF16).astype(F32)
    lo = r1 - mid
    return jnp.concatenate([hi, mid, lo, jnp.zeros_like(x)], axis=0).astype(BF16)


def _mla_in_kernel(x_ref, g_ref, w_ref, qn_ref, kvn_ref, cr_ref, sr_ref, cq_ref, ckv_ref, kr_ref):
    hn = _rms(x_ref[...], g_ref[...]).astype(BF16)
    c = _dot(hn, w_ref[...])
    cq_ref[...] = _rms(c[:, :MLA_Q_RANK], qn_ref[...]).astype(BF16)
    ckv_ref[...] = _rms(c[:, MLA_Q_RANK:MLA_Q_RANK + MLA_KV_RANK], kvn_ref[...]).astype(BF16)
    o = MLA_Q_RANK + MLA_KV_RANK
    kr_ref[...] = (c[:, o:o + LANES] * cr_ref[...] + c[:, o + LANES:o + 2 * LANES] * sr_ref[...]).astype(BF16)


def _mla_in(x, g, w_in_p, q_norm, kv_norm, cr, sr):
    t, d = x.shape
    bm = _blk(t, 512)
    wn = w_in_p.shape[1]
    row = lambda i: (i, 0)
    fixed = lambda i: (0, 0)
    return pl.pallas_call(
        _mla_in_kernel,
        grid=(t // bm,),
        in_specs=[
            pl.BlockSpec((bm, d), row),
            _resident((1, d), fixed),
            _resident((d, wn), fixed),
            _resident((1, MLA_Q_RANK), fixed),
            _resident((1, MLA_KV_RANK), fixed),
            pl.BlockSpec((bm, LANES), row),
            pl.BlockSpec((bm, LANES), row),
        ],
        out_specs=[
            pl.BlockSpec((bm, MLA_Q_RANK), row),
            pl.BlockSpec((bm, MLA_KV_RANK), row),
            pl.BlockSpec((bm, LANES), row),
        ],
        out_shape=[
            jax.ShapeDtypeStruct((t, MLA_Q_RANK), BF16),
            jax.ShapeDtypeStruct((t, MLA_KV_RANK), BF16),
            jax.ShapeDtypeStruct((t, LANES), BF16),
        ],
        compiler_params=_cparams("parallel"),
        name="mla_in",
    )(x, g, w_in_p, q_norm, kv_norm, cr, sr)


def _mla_up_kernel(cq_ref, ckv_ref, kr_ref, wq_ref, wqs_ref, wk_ref, wv_ref, cr_ref, sr_ref,
                   q_ref, k_ref, v_ref, *, scale):
    cq = cq_ref[...]
    ckv = ckv_ref[...]
    a = _dot(cq, wq_ref[...])
    b = _dot(cq, wqs_ref[...])
    kn = _dot(ckv, wk_ref[...])
    v_ref[...] = _dot(ckv, wv_ref[...]).astype(BF16)
    cr = cr_ref[...] * scale
    sr = sr_ref[...] * scale
    kr = kr_ref[...]
    for h in range(MLA_HEADS):
        q0 = h * QK_WIDTH
        q_ref[:, q0:q0 + LANES] = (a[:, q0:q0 + LANES] * scale).astype(BF16)
        q_ref[:, q0 + LANES:q0 + QK_WIDTH] = (
            a[:, q0 + LANES:q0 + QK_WIDTH] * cr + b[:, h * LANES:(h + 1) * LANES] * sr).astype(BF16)
        k_ref[:, q0:q0 + LANES] = kn[:, h * LANES:(h + 1) * LANES].astype(BF16)
        k_ref[:, q0 + LANES:q0 + QK_WIDTH] = kr


def _mla_up(cq, ckv, kr, wq, wqs, wk, wv, cr, sr):
    t = cq.shape[0]
    bm = _blk(t, 256)
    row = lambda i: (i, 0)
    fixed = lambda i: (0, 0)
    hq = MLA_HEADS * QK_WIDTH
    hv = MLA_HEADS * MLA_V
    scale = float((MLA_NOPE + MLA_ROPE) ** -0.5) * LOG2_E
    return pl.pallas_call(
        functools.partial(_mla_up_kernel, scale=scale),
        grid=(t // bm,),
        in_specs=[
            pl.BlockSpec((bm, MLA_Q_RANK), row),
            pl.BlockSpec((bm, MLA_KV_RANK), row),
            pl.BlockSpec((bm, LANES), row),
            _resident(wq.shape, fixed),
            _resident(wqs.shape, fixed),
            _resident(wk.shape, fixed),
            _resident(wv.shape, fixed),
            pl.BlockSpec((bm, LANES), row),
            pl.BlockSpec((bm, LANES), row),
        ],
        out_specs=[
            pl.BlockSpec((bm, hq), row),
            pl.BlockSpec((bm, hq), row),
            pl.BlockSpec((bm, hv), row),
        ],
        out_shape=[
            jax.ShapeDtypeStruct((t, hq), BF16),
            jax.ShapeDtypeStruct((t, hq), BF16),
            jax.ShapeDtypeStruct((t, hv), BF16),
        ],
        compiler_params=_cparams("parallel"),
        name="mla_up",
    )(cq, ckv, kr, wq, wqs, wk, wv, cr, sr)


def _flash_kernel(q_ref, k_ref, v_ref, *rest, tq, tk, rows, ncast):
    w_refs = rest[:ncast]
    o_ref = rest[ncast]
    wo_refs = rest[ncast + 1:2 * ncast + 1]
    m_scr, l_scr, acc_scr = rest[2 * ncast + 1:]
    for w_ref, wo_ref in zip(w_refs, wo_refs):
        wo_ref[...] = w_ref[...].astype(BF16)

    i = pl.program_id(2)
    m_scr[...] = jnp.full(m_scr.shape, -jnp.inf, F32)
    l_scr[...] = jnp.zeros(l_scr.shape, F32)
    acc_scr[...] = jnp.zeros(acc_scr.shape, F32)

    def update(r, k0, width, masked):
        rs = slice(r * rows, (r + 1) * rows)
        s = _dot_nt(q_ref[0, rs, :], k_ref[0, pl.ds(k0, width), :])
        if masked:
            qrel = r * rows + lax.broadcasted_iota(I32, (rows, width), 0)
            krel = lax.broadcasted_iota(I32, (rows, width), 1)
            s = jnp.where(krel <= qrel, s, -jnp.inf)
        m_prev = m_scr[rs, :]
        m_new = jnp.maximum(m_prev, jnp.max(s, axis=1, keepdims=True))
        alpha = jnp.exp2(m_prev - m_new)
        p = jnp.exp2(s - _tile_lanes(m_new, width // LANES))
        l_scr[rs, :] = alpha * l_scr[rs, :] + jnp.sum(p, axis=1, keepdims=True)
        acc_scr[rs, :] = alpha * acc_scr[rs, :] + _dot(p.astype(BF16), v_ref[0, pl.ds(k0, width), :])
        m_scr[rs, :] = m_new

    def body(j, carry):
        for r in range(tq // rows):
            update(r, pl.multiple_of(j * tk, tk), tk, False)
        return carry

    lax.fori_loop(0, i * (tq // tk), body, 0)
    for r in range(tq // rows):
        update(r, pl.multiple_of(i * tq, tq), (r + 1) * rows, True)
    o_ref[0] = (acc_scr[...] / l_scr[...]).astype(o_ref.dtype)


def _cast_rows(nrows, nsteps):
    for br in range(BF16_SUBLANES, nrows + 1, BF16_SUBLANES):
        if nrows % br == 0 and (nsteps * br) % nrows == 0 and nsteps * br >= nrows:
            return br
    raise ValueError((nrows, nsteps))


def _flash(q, k, v, bsz, s, casts):
    assert MLA_V == LANES
    tq = _blk(s, 1024)
    tk = _blk(tq, 512)
    rows = _blk(tq, 256)
    nq = s // tq
    nsteps = bsz * MLA_HEADS * nq
    cast_specs, cast_shapes = [], []
    for w, cols in casts:
        br = _cast_rows(w.shape[0], nsteps)
        per_block = nsteps * br // w.shape[0]
        spec = pl.BlockSpec((br, cols), lambda b, h, i, pb=per_block: (((b * MLA_HEADS + h) * nq + i) // pb, 0))
        cast_specs.append(spec)
        cast_shapes.append(jax.ShapeDtypeStruct((w.shape[0], cols), BF16))
    outs = pl.pallas_call(
        functools.partial(_flash_kernel, tq=tq, tk=tk, rows=rows, ncast=len(casts)),
        grid=(bsz, MLA_HEADS, nq),
        in_specs=[
            pl.BlockSpec((1, tq, QK_WIDTH), lambda b, h, i: (b, i, h)),
            pl.BlockSpec((1, s, QK_WIDTH), lambda b, h, i: (b, 0, h)),
            pl.BlockSpec((1, s, MLA_V), lambda b, h, i: (b, 0, h)),
        ] + cast_specs,
        out_specs=[pl.BlockSpec((1, tq, MLA_V), lambda b, h, i: (b, i, h))] + cast_specs,
        out_shape=[jax.ShapeDtypeStruct((bsz, s, MLA_HEADS * MLA_V), BF16)] + cast_shapes,
        scratch_shapes=[
            pltpu.VMEM((tq, LANES), F32),
            pltpu.VMEM((tq, LANES), F32),
            pltpu.VMEM((tq, MLA_V), F32),
        ],
        compiler_params=_cparams("arbitrary", "arbitrary", "arbitrary"),
        name="mla_flash",
    )(q, k, v, *[w for w, _ in casts])
    return outs[0], outs[1:]


def _proj_res_norm_kernel(a_ref, w_ref, x_ref, g_ref, xo_ref, hn_ref):
    xo = x_ref[...] + _dot(a_ref[...], w_ref[...])
    xo_ref[...] = xo
    hn_ref[...] = _rms(xo, g_ref[...]).astype(hn_ref.dtype)


def _proj_res_norm(a, w, x, g):
    t, kdim = a.shape
    d = w.shape[1]
    bm = _blk(t, 512)
    row = lambda i: (i, 0)
    fixed = lambda i: (0, 0)
    return pl.pallas_call(
        _proj_res_norm_kernel,
        grid=(t // bm,),
        in_specs=[
            pl.BlockSpec((bm, kdim), row),
            _resident((kdim, d), fixed),
            pl.BlockSpec((bm, d), row),
            _resident((1, d), fixed),
        ],
        out_specs=[pl.BlockSpec((bm, d), row), pl.BlockSpec((bm, d), row)],
        out_shape=[
            jax.ShapeDtypeStruct((t, d), F32),
            jax.ShapeDtypeStruct((t, d), BF16),
        ],
        compiler_params=_cparams("parallel"),
        name="proj_res_norm",
    )(a, w, x, g)


def _ffn_kernel(hn_ref, x_ref, wg_ref, wu_ref, wd_ref, o_ref):
    f = pl.program_id(1)

    @pl.when(f == 0)
    def _():
        o_ref[...] = x_ref[...]

    hn = hn_ref[...]
    h = (_silu(_dot(hn, wg_ref[...])) * _dot(hn, wu_ref[...])).astype(BF16)
    o_ref[...] += _dot(h, wd_ref[...])


def _ffn(hn, x, wg, wu, wd):
    t, d = x.shape
    fdim = wg.shape[1]
    bm = _blk(t, 512)
    bf = _blk(fdim, 1024)
    return pl.pallas_call(
        _ffn_kernel,
        grid=(t // bm, fdim // bf),
        in_specs=[
            pl.BlockSpec((bm, d), lambda i, f: (i, 0)),
            pl.BlockSpec((bm, d), lambda i, f: (i, 0), pipeline_mode=pl.Buffered(1)),
            pl.BlockSpec((d, bf), lambda i, f: (0, f)),
            pl.BlockSpec((d, bf), lambda i, f: (0, f)),
            pl.BlockSpec((bf, d), lambda i, f: (f, 0)),
        ],
        out_specs=pl.BlockSpec((bm, d), lambda i, f: (i, 0)),
        out_shape=jax.ShapeDtypeStruct((t, d), F32),
        compiler_params=_cparams("parallel", "arbitrary"),
        name="dense_ffn",
    )(hn, x, wg, wu, wd)


def _ssd_in_kernel(x_ref, g_ref, w_ref, wdt_ref, dtb_ref, zx_ref, dtt_ref, hn_scr):
    j = pl.program_id(1)

    @pl.when(j == 0)
    def _():
        hn = _rms(x_ref[...], g_ref[...]).astype(BF16)
        hn_scr[...] = hn
        raw = _dot_nt(wdt_ref[...], hn) + dtb_ref[...]
        dtt_ref[...] = jnp.maximum(raw, 0.0) + jnp.log1p(jnp.exp(-jnp.abs(raw)))

    zx_ref[...] = _dot(hn_scr[...], w_ref[...]).astype(zx_ref.dtype)


def _ssd_in(x, g, w_main, w_dt_t, dt_bias_col):
    t, d = x.shape
    n = w_main.shape[1]
    bm = _blk(t, 1024)
    bn = _blk(n, 1024)
    hp = w_dt_t.shape[0]
    return pl.pallas_call(
        _ssd_in_kernel,
        grid=(t // bm, n // bn),
        in_specs=[
            pl.BlockSpec((bm, d), lambda i, j: (i, 0)),
            _resident((1, d), lambda i, j: (0, 0)),
            pl.BlockSpec((d, bn), lambda i, j: (0, j)),
            _resident((hp, d), lambda i, j: (0, 0)),
            _resident((hp, 1), lambda i, j: (0, 0)),
        ],
        out_specs=[
            pl.BlockSpec((bm, bn), lambda i, j: (i, j)),
            pl.BlockSpec((hp, bm), lambda i, j: (0, i)),
        ],
        out_shape=[
            jax.ShapeDtypeStruct((t, n), BF16),
            jax.ShapeDtypeStruct((hp, t), F32),
        ],
        scratch_shapes=[pltpu.VMEM((bm, d), BF16)],
        compiler_params=_cparams("parallel", "arbitrary"),
        name="ssd_in",
    )(x, g, w_main, w_dt_t, dt_bias_col)


def _ssd_kernel(z_ref, x_ref, b_ref, c_ref, dtt_ref, alog_ref, cwx_ref, cwb_ref, cwc_ref,
                cbx_ref, cbb_ref, cbc_ref, dsk_ref, ng_ref, y_ref, ubuf, state_scr, *, L, gw):
    c_idx = pl.program_id(2)
    hg = gw // SSD_HEADDIM
    n = SSD_STATE
    tail = 8

    @pl.when(c_idx == 0)
    def _():
        ubuf[0:tail, :] = jnp.zeros((tail, ubuf.shape[1]), F32)
        state_scr[...] = jnp.zeros(state_scr.shape, F32)

    ubuf[tail:tail + L, 0:gw] = x_ref[...].astype(F32)
    ubuf[tail:tail + L, gw:gw + n] = b_ref[...].astype(F32)
    ubuf[tail:tail + L, gw + n:gw + 2 * n] = c_ref[...].astype(F32)

    cw = jnp.concatenate([cwx_ref[...], cwb_ref[...], cwc_ref[...]], axis=1)
    cb = jnp.concatenate([cbx_ref[...], cbb_ref[...], cbc_ref[...]], axis=1)
    conv = cb + cw[SSD_CONV - 1:SSD_CONV, :] * ubuf[tail:tail + L, :]
    for kk in range(SSD_CONV - 1):
        off = tail - (SSD_CONV - 1) + kk
        conv = conv + cw[kk:kk + 1, :] * ubuf[off:off + L, :]
    ubuf[0:tail, :] = ubuf[L:L + tail, :]
    u = _silu(conv)
    xc = u[:, 0:gw]
    bc = u[:, gw:gw + n].astype(BF16)
    cc = u[:, gw + n:gw + 2 * n].astype(BF16)

    dtt = dtt_ref[...]
    a_t = dtt * (-LOG2_E * jnp.exp(alog_ref[...]))
    ri = lax.broadcasted_iota(I32, (L, L), 0)
    ci = lax.broadcasted_iota(I32, (L, L), 1)
    causal = ri >= ci
    tril = jnp.where(causal, 1.0, 0.0).astype(BF16)
    r3 = _dot_nt(_split3(a_t), tril)
    acum_row = r3[0:hg] + r3[hg:2 * hg] + r3[2 * hg:3 * hg]

    def expansion(width):
        hrow = lax.broadcasted_iota(I32, (4 * hg, hg * width), 0)
        hlane = lax.broadcasted_iota(I32, (4 * hg, hg * width), 1) // width
        return jnp.where((hrow % hg == hlane) & (hrow < 3 * hg), 1.0, 0.0).astype(BF16)

    acum3 = _split3(acum_row)
    dt_e = _dot_tn(_split3(dtt), expansion(SSD_HEADDIM))
    acum_e = _dot_tn(acum3, expansion(SSD_HEADDIM))
    acum_c = _dot_tn(acum3, expansion(LANES))
    last_e = acum_e[L - 1:L, :]

    xdt = xc * dt_e
    cbm = jnp.where(causal, _dot_nt(cc, bc), 0.0)

    st = state_scr[...]
    y = _dot(cc, st.astype(BF16)) * jnp.exp2(acum_e) + xc * dsk_ref[...]

    lane = lax.broadcasted_iota(I32, (L, LANES), 1)
    pieces = []
    for pair in range(hg // 2):
        xp = xdt[:, pair * LANES:(pair + 1) * LANES]
        yp = None
        for sub in range(2):
            jh = 2 * pair + sub
            col = _tile_lanes(acum_c[:, jh * LANES:(jh + 1) * LANES], L // LANES)
            seg = jnp.minimum(col - acum_row[jh:jh + 1, :], 0.0)
            m = (cbm * jnp.exp2(seg)).astype(BF16)
            keep = (lane // SSD_HEADDIM) == sub
            d = _dot(m, jnp.where(keep, xp, 0.0).astype(BF16))
            yp = d if yp is None else yp + d
        pieces.append(yp)
    y = y + jnp.concatenate(pieces, axis=1)

    xw = (xdt * jnp.exp2(last_e - acum_e)).astype(BF16)
    state_scr[...] = st * jnp.exp2(last_e) + _dot_tn(bc, xw)

    zf = z_ref[...].astype(F32)
    yg = y * _silu(zf)
    yg = yg * lax.rsqrt(jnp.mean(yg * yg, axis=-1, keepdims=True) + RMS_EPS)
    y_ref[...] = (yg * ng_ref[...]).astype(y_ref.dtype)


def _ssd_core(zx, dtt, a_log_col, conv_w, conv_b, d_exp, norm_g, bsz, s, inner):
    t = zx.shape[0]
    L = SSD_CHUNK
    assert s % L == 0
    nc = s // L
    gw = inner // SSD_GROUPS
    hg = gw // SSD_HEADDIM
    n = SSD_STATE
    gn = SSD_GROUPS * n
    xb0 = inner // gw
    bb0 = (2 * inner) // n
    cb0 = (2 * inner + gn) // n
    cwb0 = inner // n
    cwc0 = (inner + gn) // n
    rowi = lambda b, g, c: b * nc + c
    return pl.pallas_call(
        functools.partial(_ssd_kernel, L=L, gw=gw),
        grid=(bsz, SSD_GROUPS, nc),
        in_specs=[
            pl.BlockSpec((L, gw), lambda b, g, c: (rowi(b, g, c), g)),
            pl.BlockSpec((L, gw), lambda b, g, c: (rowi(b, g, c), xb0 + g)),
            pl.BlockSpec((L, n), lambda b, g, c: (rowi(b, g, c), bb0 + g)),
            pl.BlockSpec((L, n), lambda b, g, c: (rowi(b, g, c), cb0 + g)),
            pl.BlockSpec((hg, L), lambda b, g, c: (g, rowi(b, g, c))),
            pl.BlockSpec((hg, 1), lambda b, g, c: (g, 0)),
            pl.BlockSpec((SSD_CONV, gw), lambda b, g, c: (0, g)),
            pl.BlockSpec((SSD_CONV, n), lambda b, g, c: (0, cwb0 + g)),
            pl.BlockSpec((SSD_CONV, n), lambda b, g, c: (0, cwc0 + g)),
            pl.BlockSpec((1, gw), lambda b, g, c: (0, g)),
            pl.BlockSpec((1, n), lambda b, g, c: (0, cwb0 + g)),
            pl.BlockSpec((1, n), lambda b, g, c: (0, cwc0 + g)),
            pl.BlockSpec((1, gw), lambda b, g, c: (0, g)),
            pl.BlockSpec((1, gw), lambda b, g, c: (0, g)),
        ],
        out_specs=pl.BlockSpec((L, gw), lambda b, g, c: (rowi(b, g, c), g)),
        out_shape=jax.ShapeDtypeStruct((t, inner), BF16),
        scratch_shapes=[
            pltpu.VMEM((L + 8, gw + 2 * n), F32),
            pltpu.VMEM((n, gw), F32),
        ],
        compiler_params=_cparams("parallel", "parallel", "arbitrary"),
        name="ssd_core",
    )(zx, zx, zx, zx, dtt, a_log_col, conv_w, conv_w, conv_w, conv_b, conv_b, conv_b, d_exp, norm_g)


def _ssd_out_router_kernel(a_ref, w_ref, x_ref, g_ref, r_ref, xo_ref, hnp_ref, gate_ref, meta_ref, cnt_ref,
                           run_scr):
    bm, d = xo_ref.shape

    @pl.when(pl.program_id(0) == 0)
    def _():
        run_scr[...] = jnp.zeros(run_scr.shape, F32)

    xo = x_ref[...] + _dot(a_ref[...], w_ref[...])
    xo_ref[...] = xo

    hn = _rms(xo, g_ref[...])
    hn_hi = hn.astype(BF16)
    bits = pltpu.bitcast(hn_hi.astype(F32), U32)
    hnp_ref[...] = (bits[:, d // 2:] & jnp.uint32(0xFFFF0000)) | (bits[:, :d // 2] >> 16)

    hn_lo = (hn - hn_hi.astype(F32)).astype(BF16)
    rt = r_ref[...]
    rt_hi = rt.astype(BF16)
    rt_lo = (rt - rt_hi.astype(F32)).astype(BF16)
    logits = _dot(hn_hi, rt_hi) + (_dot(hn_lo, rt_hi) + _dot(hn_hi, rt_lo))
    lane = lax.broadcasted_iota(I32, logits.shape, 1)
    logits = jnp.where(lane < N_EXPERTS, logits, -jnp.inf)
    m1 = jnp.max(logits, axis=1, keepdims=True)
    i1 = jnp.min(jnp.where(logits == m1, lane, LANES), axis=1, keepdims=True)
    rest = jnp.where(lane == i1, -jnp.inf, logits)
    m2 = jnp.max(rest, axis=1, keepdims=True)
    i2 = jnp.min(jnp.where(rest == m2, lane, LANES), axis=1, keepdims=True)
    e21 = jnp.exp(m2 - m1)
    g1 = 1.0 / (1.0 + e21)
    g2 = e21 / (1.0 + e21)
    gate_ref[...] = jnp.where(lane == 0, g1, jnp.where(lane == 1, g2, 0.0))

    oh1 = lane == i1
    oh2 = lane == i2
    both = (oh1 | oh2).astype(BF16)
    ri = lax.broadcasted_iota(I32, (bm, bm), 0)
    ci = lax.broadcasted_iota(I32, (bm, bm), 1)
    before = _dot((ri > ci).astype(BF16), both) + run_scr[...]
    r1 = jnp.sum(jnp.where(oh1, before, 0.0), axis=1, keepdims=True)
    r2 = jnp.sum(jnp.where(oh2, before, 0.0), axis=1, keepdims=True)
    meta = jnp.where(lane == 0, i1.astype(F32),
                     jnp.where(lane == 1, i2.astype(F32),
                               jnp.where(lane == 2, r1, jnp.where(lane == 3, r2, 0.0))))
    meta_ref[...] = meta.astype(I32)
    run_scr[...] = run_scr[...] + jnp.sum(both.astype(F32), axis=0, keepdims=True)
    cnt_ref[...] = jnp.broadcast_to(run_scr[...], cnt_ref.shape).astype(I32)


def _ssd_out_router(a, w, x, g, router_p):
    t, kdim = a.shape
    d = w.shape[1]
    bm = _blk(t, 256)
    row = lambda i: (i, 0)
    fixed = lambda i: (0, 0)
    return pl.pallas_call(
        _ssd_out_router_kernel,
        grid=(t // bm,),
        in_specs=[
            pl.BlockSpec((bm, kdim), row),
            _resident((kdim, d), fixed),
            pl.BlockSpec((bm, d), row),
            _resident((1, d), fixed),
            _resident((d, LANES), fixed),
        ],
        out_specs=[
            pl.BlockSpec((bm, d), row),
            pl.BlockSpec((bm, d // 2), row),
            pl.BlockSpec((bm, LANES), row),
            pl.BlockSpec((bm, LANES), row),
            pl.BlockSpec((8, LANES), fixed),
        ],
        out_shape=[
            jax.ShapeDtypeStruct((t, d), F32),
            jax.ShapeDtypeStruct((t, d // 2), U32),
            jax.ShapeDtypeStruct((t, LANES), F32),
            jax.ShapeDtypeStruct((t, LANES), I32),
            jax.ShapeDtypeStruct((8, LANES), I32),
        ],
        scratch_shapes=[pltpu.VMEM((1, LANES), F32)],
        compiler_params=_cparams("arbitrary"),
        name="ssd_out_router",
    )(a, w, x, g, router_p)


def _dispatch_kernel(pos_ref, hn_ref, xb_in_ref, xb_ref, sem):
    del xb_in_ref
    bm = hn_ref.shape[0]

    def row_copy(r, kk):
        dst = pos_ref[0, 0, TOP_K * r + kk]
        return pltpu.make_async_copy(hn_ref.at[pl.ds(r, 1)], xb_ref.at[pl.ds(dst, 1)], sem)

    def start(r, carry):
        for kk in range(TOP_K):
            row_copy(r, kk).start()
        return carry

    def wait(r, carry):
        for kk in range(TOP_K):
            row_copy(r, kk).wait()
        return carry

    lax.fori_loop(0, bm, start, 0, unroll=8)
    lax.fori_loop(0, bm, wait, 0, unroll=8)


def _dispatch(hnp, pos_blocks, cap):
    t, w = hnp.shape
    nb, _, two_bm = pos_blocks.shape
    bm = two_bm // TOP_K
    xb_init = jnp.zeros((cap, w), U32)
    return pl.pallas_call(
        _dispatch_kernel,
        grid=(nb,),
        in_specs=[
            pl.BlockSpec((1, 1, two_bm), lambda i: (i, 0, 0), memory_space=pltpu.SMEM),
            pl.BlockSpec((bm, w), lambda i: (i, 0)),
            pl.BlockSpec(memory_space=pl.ANY),
        ],
        out_specs=pl.BlockSpec(memory_space=pl.ANY),
        out_shape=jax.ShapeDtypeStruct((cap, w), U32),
        scratch_shapes=[pltpu.SemaphoreType.DMA(())],
        input_output_aliases={2: 0},
        compiler_params=_cparams("arbitrary"),
        name="moe_dispatch",
    )(pos_blocks, hnp, xb_init)


def _moe_kernel(be_ref, nu_ref, xb_ref, wg_ref, wu_ref, wd_ref, o_ref, xs_scr):
    del be_ref
    b = pl.program_id(0)
    f = pl.program_id(1)

    @pl.when(f == 0)
    def _():
        o_ref[...] = jnp.zeros(o_ref.shape, F32)

    @pl.when(b < nu_ref[0])
    def _():
        @pl.when(f == 0)
        def _():
            w = xb_ref[...]
            half = w.shape[1]
            xs_scr[:, :half] = pltpu.bitcast(w << 16, F32).astype(BF16)
            xs_scr[:, half:] = pltpu.bitcast(w & jnp.uint32(0xFFFF0000), F32).astype(BF16)

        xs = xs_scr[...]
        h = (_silu(_dot(xs, wg_ref[0])) * _dot(xs, wu_ref[0])).astype(BF16)
        o_ref[...] += _dot(h, wd_ref[0])


def _moe(xb, block_expert, n_used, wg, wu, wd, bm):
    cap, half = xb.shape
    d = 2 * half
    fdim = wg.shape[2]
    bf = _blk(fdim, 1024)
    nf = fdim // bf
    nb = cap // bm

    def live(b, nu):
        return jnp.maximum(jnp.minimum(b, nu[0] - 1), 0)

    def f_eff(b, f, nu):
        return jnp.where(b < nu[0], f, nf - 1)

    grid_spec = pltpu.PrefetchScalarGridSpec(
        num_scalar_prefetch=2,
        grid=(nb, nf),
        in_specs=[
            pl.BlockSpec((bm, half), lambda b, f, be, nu: (live(b, nu), 0)),
            pl.BlockSpec((1, d, bf), lambda b, f, be, nu: (be[live(b, nu)], 0, f_eff(b, f, nu))),
            pl.BlockSpec((1, d, bf), lambda b, f, be, nu: (be[live(b, nu)], 0, f_eff(b, f, nu))),
            pl.BlockSpec((1, bf, d), lambda b, f, be, nu: (be[live(b, nu)], f_eff(b, f, nu), 0)),
        ],
        out_specs=pl.BlockSpec((bm, d), lambda b, f, be, nu: (b, 0)),
        scratch_shapes=[pltpu.VMEM((bm, d), BF16)],
    )
    return pl.pallas_call(
        _moe_kernel,
        grid_spec=grid_spec,
        out_shape=jax.ShapeDtypeStruct((cap, d), F32),
        compiler_params=_cparams("arbitrary", "arbitrary"),
        name="moe_ffn",
    )(block_expert, n_used, xb, wg, wu, wd)


def _combine_kernel(pos_ref, x_ref, gate_ref, g_ref, yb_ref, o_ref, ybuf, sem):
    bm = x_ref.shape[0]

    def row_copy(r, kk):
        src = pos_ref[0, 0, TOP_K * r + kk]
        return pltpu.make_async_copy(yb_ref.at[pl.ds(src, 1)], ybuf.at[kk, pl.ds(r, 1)], sem)

    def start(r, carry):
        for kk in range(TOP_K):
            row_copy(r, kk).start()
        return carry

    def wait(r, carry):
        for kk in range(TOP_K):
            row_copy(r, kk).wait()
        return carry

    lax.fori_loop(0, bm, start, 0, unroll=8)
    lax.fori_loop(0, bm, wait, 0, unroll=8)
    gates = gate_ref[...]
    x = x_ref[...] + gates[:, 0:1] * ybuf[0] + gates[:, 1:2] * ybuf[1]
    o_ref[...] = _rms(x, g_ref[...])


def _combine(x, gates, g_final, yb, pos_blocks):
    t, d = x.shape
    nb, _, two_bm = pos_blocks.shape
    bm = two_bm // TOP_K
    return pl.pallas_call(
        _combine_kernel,
        grid=(nb,),
        in_specs=[
            pl.BlockSpec((1, 1, two_bm), lambda i: (i, 0, 0), memory_space=pltpu.SMEM),
            pl.BlockSpec((bm, d), lambda i: (i, 0)),
            pl.BlockSpec((bm, LANES), lambda i: (i, 0)),
            _resident((1, d), lambda i: (0, 0)),
            pl.BlockSpec(memory_space=pl.ANY),
        ],
        out_specs=pl.BlockSpec((bm, d), lambda i: (i, 0)),
        out_shape=jax.ShapeDtypeStruct((t, d), F32),
        scratch_shapes=[pltpu.VMEM((TOP_K, bm, d), F32), pltpu.SemaphoreType.DMA(())],
        compiler_params=_cparams("arbitrary"),
        name="moe_combine",
    )(pos_blocks, x, gates, g_final, yb)


def _rope_tables(positions):
    inv_freq = ROPE_THETA ** (-jnp.arange(0, MLA_ROPE, 2, dtype=F32) / MLA_ROPE)
    ang = positions.astype(F32).reshape(-1)[:, None] * inv_freq
    cos, sin = jnp.cos(ang), jnp.sin(ang)
    zeros = jnp.zeros((ang.shape[0], LANES - MLA_ROPE), F32)
    return jnp.concatenate([cos, cos, zeros], axis=1), jnp.concatenate([-sin, sin, zeros], axis=1)


def _swap_halves(w):
    half = w.shape[-1] // 2
    return jnp.concatenate([w[..., half:], w[..., :half]], axis=-1)


def _mla_weights(w_in, w_uq, w_ukv):
    d = w_in.shape[0]
    o = MLA_Q_RANK + MLA_KV_RANK
    w_kr = w_in[:, o:]
    zpad = jnp.zeros((d, LANES - MLA_ROPE), w_in.dtype)
    w_in_p = jnp.concatenate([w_in[:, :o], w_kr, zpad, _swap_halves(w_kr), zpad], axis=1).astype(BF16)
    uq = w_uq.reshape(MLA_Q_RANK, MLA_HEADS, MLA_NOPE + MLA_ROPE)
    rope = uq[:, :, MLA_NOPE:]
    zq = jnp.zeros((MLA_Q_RANK, MLA_HEADS, LANES - MLA_ROPE), w_uq.dtype)
    wq = jnp.concatenate([uq[:, :, :MLA_NOPE], rope, zq], axis=2).reshape(MLA_Q_RANK, -1).astype(BF16)
    wqs = jnp.concatenate([_swap_halves(rope), zq], axis=2).reshape(MLA_Q_RANK, -1).astype(BF16)
    ukv = w_ukv.reshape(MLA_KV_RANK, MLA_HEADS, MLA_NOPE + MLA_V)
    wk = ukv[:, :, :MLA_NOPE].reshape(MLA_KV_RANK, -1).astype(BF16)
    wv = ukv[:, :, MLA_NOPE:].reshape(MLA_KV_RANK, -1).astype(BF16)
    return w_in_p, wq, wqs, wk, wv


def kernel(x, positions, mla_w_in, mla_q_norm, mla_kv_norm, mla_w_uq, mla_w_ukv, mla_w_o, ssd_w_in, ssd_conv_w, ssd_conv_b, ssd_dt_bias, ssd_a_log, ssd_d, ssd_norm, ssd_w_o, ffn_w_gate, ffn_w_up, ffn_w_down, moe_router, moe_w_gate, moe_w_up, moe_w_down, norm_mix, norm_ffn, norm_final):
    bsz, s, d = x.shape
    t = bsz * s
    assert norm_mix.shape[0] == 2 and mla_w_in.shape[0] == 1 and ssd_w_in.shape[0] == 1
    xf = x.reshape(t, d)
    cr, sr = _rope_tables(positions)
    row = lambda v: v.reshape(1, -1)
    bf = lambda v: v.astype(BF16)

    w_in_p, wq, wqs, wk, wv = _mla_weights(mla_w_in[0], mla_w_uq[0], mla_w_ukv[0])
    cq, ckv, kr = _mla_in(xf, row(norm_mix[0]), w_in_p, row(mla_q_norm[0]), row(mla_kv_norm[0]), cr, sr)
    q, k, v = _mla_up(cq, ckv, kr, wq, wqs, wk, wv, cr, sr)
    inner = ssd_norm.shape[1]
    heads = inner // SSD_HEADDIM
    n_main = 2 * inner + 2 * SSD_GROUPS * SSD_STATE
    w_in = ssd_w_in[0]
    fdim = ffn_w_gate.shape[2]
    flat = lambda w: w.reshape(-1, w.shape[-1])
    full = lambda w: (flat(w), w.shape[-1])
    casts = [full(mla_w_o[0]), full(ffn_w_gate[0]), full(ffn_w_up[0]), full(ffn_w_down[0]),
             (w_in, n_main), full(ssd_w_o[0]), full(moe_w_gate[0]), full(moe_w_up[0]), full(moe_w_down[0])]
    o, (w_o_b, fg_b, fu_b, fd_b, ssd_in_b, ssd_o_b, mg_b, mu_b, md_b) = _flash(
        q.reshape(bsz, s, -1), k.reshape(bsz, s, -1), v.reshape(bsz, s, -1), bsz, s, casts)
    xf, hn = _proj_res_norm(o.reshape(t, -1), w_o_b, xf, row(norm_ffn[0]))
    xf = _ffn(hn, xf, fg_b, fu_b, fd_b)

    w_dt_t = bf(jnp.zeros((LANES, d), F32).at[:heads].set(w_in[:, n_main:].T))
    dtb = jnp.zeros((LANES, 1), F32).at[:heads, 0].set(ssd_dt_bias[0])
    zx, dtt = _ssd_in(xf, row(norm_mix[1]), ssd_in_b, w_dt_t, dtb)
    a_log_col = jnp.zeros((LANES, 1), F32).at[:heads, 0].set(ssd_a_log[0])
    d_exp = jnp.repeat(ssd_d[0], SSD_HEADDIM).reshape(1, inner)
    y = _ssd_core(zx, dtt, a_log_col, ssd_conv_w[0], row(ssd_conv_b[0]), d_exp, row(ssd_norm[0]), bsz, s, inner)
    router_p = jnp.zeros((d, LANES), F32).at[:, :N_EXPERTS].set(moe_router[0])
    xf, hnp, gates, meta, cnt = _ssd_out_router(y, ssd_o_b, xf, row(norm_ffn[1]), router_p)

    n_blocks = -(-(t * TOP_K) // MOE_ROWS) + N_EXPERTS
    counts = cnt[0, :N_EXPERTS]
    padded = ((counts + MOE_ROWS - 1) // MOE_ROWS) * MOE_ROWS
    pad_ends = jnp.cumsum(padded)
    pad_starts = pad_ends - padded
    pos = pad_starts[meta[:, 0:TOP_K]] + meta[:, TOP_K:2 * TOP_K]
    n_used = (pad_ends[-1] // MOE_ROWS).astype(I32).reshape(1)
    block_expert = jnp.minimum(
        jnp.searchsorted(pad_ends, jnp.arange(n_blocks, dtype=I32) * MOE_ROWS, side='right'),
        N_EXPERTS - 1).astype(I32)
    bm_d = _blk(t, 256)
    pos_blocks = pos.astype(I32).reshape(t // bm_d, 1, TOP_K * bm_d)
    xb = _dispatch(hnp, pos_blocks, n_blocks * MOE_ROWS)
    yb = _moe(xb, block_expert, n_used, mg_b.reshape(N_EXPERTS, d, fdim), mu_b.reshape(N_EXPERTS, d, fdim),
              md_b.reshape(N_EXPERTS, fdim, d), MOE_ROWS)
    out = _combine(xf, gates, row(norm_final), yb, pos_blocks)
    return out.reshape(bsz, s, d)
```

```python
import functools

import jax
import jax.numpy as jnp
from jax import lax
from jax.experimental import pallas as pl
from jax.experimental.pallas import tpu as pltpu

F32 = jnp.float32
BF16 = jnp.bfloat16
U32 = jnp.uint32
I32 = jnp.int32

RMS_EPS = 1e-6
MLA_HEADS = 16
MLA_Q_RANK = 512
MLA_KV_RANK = 512
MLA_NOPE = 128
MLA_ROPE = 64
MLA_V = 128
ROPE_THETA = 10000.0
SSD_HEADDIM = 64
SSD_GROUPS = 8
SSD_STATE = 128
SSD_CONV = 4
SSD_CHUNK = 256
N_EXPERTS = 8
TOP_K = 2
MOE_ROWS = 512
FLASH_TQ = 1024
FLASH_TK = 1024
FLASH_ROWS = 512

LANES = 128
BF16_SUBLANES = 16
QK_WIDTH = 2 * LANES
VMEM_LIMIT_BYTES = 56 * 1024 * 1024
LOG2_E = 1.4426950408889634


def _cparams(*sem):
    return pltpu.CompilerParams(dimension_semantics=sem, vmem_limit_bytes=VMEM_LIMIT_BYTES)


def _blk(n, pref):
    b = min(n, pref)
    assert n % b == 0, (n, pref)
    return b


def _resident(shape, index_map):
    return pl.BlockSpec(shape, index_map, pipeline_mode=pl.Buffered(1))


def _rms(xf, g):
    r = lax.rsqrt(jnp.mean(xf * xf, axis=-1, keepdims=True) + RMS_EPS)
    return xf * r * g


def _silu(x):
    h = 0.5 * x
    return h + h * jnp.tanh(h)


def _dot(a, b):
    return jnp.dot(a, b, preferred_element_type=F32)


def _dot_nt(a, b, precision=None):
    return lax.dot_general(a, b, (((1,), (1,)), ((), ())), preferred_element_type=F32, precision=precision)


def _dot_tn(a, b):
    return lax.dot_general(a, b, (((0,), (0,)), ((), ())), preferred_element_type=F32)


def _tile_lanes(x, reps):
    return x if reps == 1 else jnp.concatenate([x] * reps, axis=1)


def _split3(x):
    hi = x.astype(BF16).astype(F32)
    r1 = x - hi
    mid = r1.astype(BF16).astype(F32)
    lo = r1 - mid
    return jnp.concatenate([hi, mid, lo, jnp.zeros_like(x)], axis=0).astype(BF16)


def _mla_in_kernel(x_ref, g_ref, w_ref, qn_ref, kvn_ref, cr_ref, sr_ref, cq_ref, ckv_ref, kr_ref):
    hn = _rms(x_ref[...], g_ref[...]).astype(BF16)
    c = _dot(hn, w_ref[...])
    cq_ref[...] = _rms(c[:, :MLA_Q_RANK], qn_ref[...]).astype(BF16)
    ckv_ref[...] = _rms(c[:, MLA_Q_RANK:MLA_Q_RANK + MLA_KV_RANK], kvn_ref[...]).astype(BF16)
    o = MLA_Q_RANK + MLA_KV_RANK
    kr_ref[...] = (c[:, o:o + LANES] * cr_ref[...] + c[:, o + LANES:o + 2 * LANES] * sr_ref[...]).astype(BF16)


def _mla_in(x, g, w_in_p, q_norm, kv_norm, cr, sr):
    t, d = x.shape
    bm = _blk(t, 512)
    wn = w_in_p.shape[1]
    row = lambda i: (i, 0)
    fixed = lambda i: (0, 0)
    return pl.pallas_call(
        _mla_in_kernel,
        grid=(t // bm,),
        in_specs=[
            pl.BlockSpec((bm, d), row),
            _resident((1, d), fixed),
            _resident((d, wn), fixed),
            _resident((1, MLA_Q_RANK), fixed),
            _resident((1, MLA_KV_RANK), fixed),
            pl.BlockSpec((bm, LANES), row),
            pl.BlockSpec((bm, LANES), row),
        ],
        out_specs=[
            pl.BlockSpec((bm, MLA_Q_RANK), row),
            pl.BlockSpec((bm, MLA_KV_RANK), row),
            pl.BlockSpec((bm, LANES), row),
        ],
        out_shape=[
            jax.ShapeDtypeStruct((t, MLA_Q_RANK), BF16),
            jax.ShapeDtypeStruct((t, MLA_KV_RANK), BF16),
            jax.ShapeDtypeStruct((t, LANES), BF16),
        ],
        compiler_params=_cparams("parallel"),
        name="mla_in",
    )(x, g, w_in_p, q_norm, kv_norm, cr, sr)


def _mla_up_kernel(cq_ref, ckv_ref, kr_ref, wq_ref, wqs_ref, wk_ref, wv_ref, cr_ref, sr_ref,
                   q_ref, k_ref, v_ref, *, scale):
    cq = cq_ref[...]
    ckv = ckv_ref[...]
    a = _dot(cq, wq_ref[...])
    b = _dot(cq, wqs_ref[...])
    kn = _dot(ckv, wk_ref[...])
    v_ref[...] = _dot(ckv, wv_ref[...]).astype(BF16)
    cr = cr_ref[...] * scale
    sr = sr_ref[...] * scale
    kr = kr_ref[...]
    for h in range(MLA_HEADS):
        q0 = h * QK_WIDTH
        q_ref[:, q0:q0 + LANES] = (a[:, q0:q0 + LANES] * scale).astype(BF16)
        q_ref[:, q0 + LANES:q0 + QK_WIDTH] = (
            a[:, q0 + LANES:q0 + QK_WIDTH] * cr + b[:, h * LANES:(h + 1) * LANES] * sr).astype(BF16)
        k_ref[:, q0:q0 + LANES] = kn[:, h * LANES:(h + 1) * LANES].astype(BF16)
        k_ref[:, q0 + LANES:q0 + QK_WIDTH] = kr


def _mla_up(cq, ckv, kr, wq, wqs, wk, wv, cr, sr):
    t = cq.shape[0]
    bm = _blk(t, 256)
    row = lambda i: (i, 0)
    fixed = lambda i: (0, 0)
    hq = MLA_HEADS * QK_WIDTH
    hv = MLA_HEADS * MLA_V
    scale = float((MLA_NOPE + MLA_ROPE) ** -0.5) * LOG2_E
    return pl.pallas_call(
        functools.partial(_mla_up_kernel, scale=scale),
        grid=(t // bm,),
        in_specs=[
            pl.BlockSpec((bm, MLA_Q_RANK), row),
            pl.BlockSpec((bm, MLA_KV_RANK), row),
            pl.BlockSpec((bm, LANES), row),
            _resident(wq.shape, fixed),
            _resident(wqs.shape, fixed),
            _resident(wk.shape, fixed),
            _resident(wv.shape, fixed),
            pl.BlockSpec((bm, LANES), row),
            pl.BlockSpec((bm, LANES), row),
        ],
        out_specs=[
            pl.BlockSpec((bm, hq), row),
            pl.BlockSpec((bm, hq), row),
            pl.BlockSpec((bm, hv), row),
        ],
        out_shape=[
            jax.ShapeDtypeStruct((t, hq), BF16),
            jax.ShapeDtypeStruct((t, hq), BF16),
            jax.ShapeDtypeStruct((t, hv), BF16),
        ],
        compiler_params=_cparams("parallel"),
        name="mla_up",
    )(cq, ckv, kr, wq, wqs, wk, wv, cr, sr)


def _flash_kernel(q_ref, k_ref, v_ref, *rest, tq, tk, rows, ncast):
    w_refs = rest[:ncast]
    o_ref = rest[ncast]
    wo_refs = rest[ncast + 1:2 * ncast + 1]
    m_scr, l_scr, acc_scr = rest[2 * ncast + 1:]
    for w_ref, wo_ref in zip(w_refs, wo_refs):
        wo_ref[...] = w_ref[...].astype(BF16)

    i = pl.program_id(2)
    m_scr[...] = jnp.full(m_scr.shape, -jnp.inf, F32)
    l_scr[...] = jnp.zeros(l_scr.shape, F32)
    acc_scr[...] = jnp.zeros(acc_scr.shape, F32)

    def update(row0, nrows, k0, width, masked):
        rs = slice(row0, row0 + nrows)
        s = _dot_nt(q_ref[0, rs, :], k_ref[0, pl.ds(k0, width), :])
        if masked:
            qrel = row0 + lax.broadcasted_iota(I32, (nrows, width), 0)
            krel = lax.broadcasted_iota(I32, (nrows, width), 1)
            s = jnp.where(krel <= qrel, s, -jnp.inf)
        m_prev = m_scr[rs, :]
        m_new = jnp.maximum(m_prev, jnp.max(s, axis=1, keepdims=True))
        alpha = jnp.exp2(m_prev - m_new)
        p = jnp.exp2(s - _tile_lanes(m_new, width // LANES))
        l_scr[rs, :] = alpha * l_scr[rs, :] + jnp.sum(p, axis=1, keepdims=True)
        acc_scr[rs, :] = alpha * acc_scr[rs, :] + _dot(p.astype(BF16), v_ref[0, pl.ds(k0, width), :])
        m_scr[rs, :] = m_new

    def body(j, carry):
        update(0, tq, pl.multiple_of(j * tk, tk), tk, False)
        return carry

    lax.fori_loop(0, i * (tq // tk), body, 0)
    for r in range(tq // rows):
        update(r * rows, rows, pl.multiple_of(i * tq, tq), (r + 1) * rows, True)
    o_ref[0] = (acc_scr[...] / l_scr[...]).astype(o_ref.dtype)


def _cast_rows(nrows, nsteps):
    for br in range(BF16_SUBLANES, nrows + 1, BF16_SUBLANES):
        if nrows % br == 0 and (nsteps * br) % nrows == 0 and nsteps * br >= nrows:
            return br
    raise ValueError((nrows, nsteps))


def _flash(q, k, v, bsz, s, casts):
    assert MLA_V == LANES
    tq = _blk(s, FLASH_TQ)
    tk = _blk(tq, FLASH_TK)
    rows = _blk(tq, FLASH_ROWS)
    nq = s // tq
    nsteps = bsz * MLA_HEADS * nq
    cast_specs, cast_shapes = [], []
    for w, cols in casts:
        br = _cast_rows(w.shape[0], nsteps)
        per_block = nsteps * br // w.shape[0]
        spec = pl.BlockSpec((br, cols), lambda b, h, i, pb=per_block: (((b * MLA_HEADS + h) * nq + i) // pb, 0))
        cast_specs.append(spec)
        cast_shapes.append(jax.ShapeDtypeStruct((w.shape[0], cols), BF16))
    outs = pl.pallas_call(
        functools.partial(_flash_kernel, tq=tq, tk=tk, rows=rows, ncast=len(casts)),
        grid=(bsz, MLA_HEADS, nq),
        in_specs=[
            pl.BlockSpec((1, tq, QK_WIDTH), lambda b, h, i: (b, i, h)),
            pl.BlockSpec((1, s, QK_WIDTH), lambda b, h, i: (b, 0, h)),
            pl.BlockSpec((1, s, MLA_V), lambda b, h, i: (b, 0, h)),
        ] + cast_specs,
        out_specs=[pl.BlockSpec((1, tq, MLA_V), lambda b, h, i: (b, i, h))] + cast_specs,
        out_shape=[jax.ShapeDtypeStruct((bsz, s, MLA_HEADS * MLA_V), BF16)] + cast_shapes,
        scratch_shapes=[
            pltpu.VMEM((tq, LANES), F32),
            pltpu.VMEM((tq, LANES), F32),
            pltpu.VMEM((tq, MLA_V), F32),
        ],
        compiler_params=_cparams("arbitrary", "arbitrary", "arbitrary"),
        name="mla_flash",
    )(q, k, v, *[w for w, _ in casts])
    return outs[0], outs[1:]


def _proj_res_norm_kernel(a_ref, w_ref, x_ref, g_ref, xo_ref, hn_ref):
    xo = x_ref[...] + _dot(a_ref[...], w_ref[...])
    xo_ref[...] = xo
    hn_ref[...] = _rms(xo, g_ref[...]).astype(hn_ref.dtype)


def _proj_res_norm(a, w, x, g):
    t, kdim = a.shape
    d = w.shape[1]
    bm = _blk(t, 512)
    row = lambda i: (i, 0)
    fixed = lambda i: (0, 0)
    return pl.pallas_call(
        _proj_res_norm_kernel,
        grid=(t // bm,),
        in_specs=[
            pl.BlockSpec((bm, kdim), row),
            _resident((kdim, d), fixed),
            pl.BlockSpec((bm, d), row),
            _resident((1, d), fixed),
        ],
        out_specs=[pl.BlockSpec((bm, d), row), pl.BlockSpec((bm, d), row)],
        out_shape=[
            jax.ShapeDtypeStruct((t, d), F32),
            jax.ShapeDtypeStruct((t, d), BF16),
        ],
        compiler_params=_cparams("parallel"),
        name="proj_res_norm",
    )(a, w, x, g)


def _ffn_kernel(hn_ref, x_ref, wg_ref, wu_ref, wd_ref, o_ref):
    f = pl.program_id(1)

    @pl.when(f == 0)
    def _():
        o_ref[...] = x_ref[...]

    hn = hn_ref[...]
    h = (_silu(_dot(hn, wg_ref[...])) * _dot(hn, wu_ref[...])).astype(BF16)
    o_ref[...] += _dot(h, wd_ref[...])


def _ffn(hn, x, wg, wu, wd):
    t, d = x.shape
    fdim = wg.shape[1]
    bm = _blk(t, 512)
    bf = _blk(fdim, 1024)
    return pl.pallas_call(
        _ffn_kernel,
        grid=(t // bm, fdim // bf),
        in_specs=[
            pl.BlockSpec((bm, d), lambda i, f: (i, 0)),
            pl.BlockSpec((bm, d), lambda i, f: (i, 0)),
            pl.BlockSpec((d, bf), lambda i, f: (0, f)),
            pl.BlockSpec((d, bf), lambda i, f: (0, f)),
            pl.BlockSpec((bf, d), lambda i, f: (f, 0)),
        ],
        out_specs=pl.BlockSpec((bm, d), lambda i, f: (i, 0)),
        out_shape=jax.ShapeDtypeStruct((t, d), F32),
        compiler_params=_cparams("parallel", "arbitrary"),
        name="dense_ffn",
    )(hn, x, wg, wu, wd)


def _ssd_in_kernel(x_ref, g_ref, w_ref, wdt_ref, dtb_ref, zx_ref, dtt_ref, hn_scr):
    j = pl.program_id(1)

    @pl.when(j == 0)
    def _():
        hn = _rms(x_ref[...], g_ref[...]).astype(BF16)
        hn_scr[...] = hn
        raw = _dot_nt(wdt_ref[...], hn) + dtb_ref[...]
        dtt_ref[...] = jnp.maximum(raw, 0.0) + jnp.log1p(jnp.exp(-jnp.abs(raw)))

    zx_ref[...] = _dot(hn_scr[...], w_ref[...]).astype(zx_ref.dtype)


def _ssd_in(x, g, w_main, w_dt_t, dt_bias_col):
    t, d = x.shape
    n = w_main.shape[1]
    bm = _blk(t, 1024)
    bn = _blk(n, 1024)
    hp = w_dt_t.shape[0]
    return pl.pallas_call(
        _ssd_in_kernel,
        grid=(t // bm, n // bn),
        in_specs=[
            pl.BlockSpec((bm, d), lambda i, j: (i, 0)),
            _resident((1, d), lambda i, j: (0, 0)),
            pl.BlockSpec((d, bn), lambda i, j: (0, j)),
            _resident((hp, d), lambda i, j: (0, 0)),
            _resident((hp, 1), lambda i, j: (0, 0)),
        ],
        out_specs=[
            pl.BlockSpec((bm, bn), lambda i, j: (i, j)),
            pl.BlockSpec((hp, bm), lambda i, j: (0, i)),
        ],
        out_shape=[
            jax.ShapeDtypeStruct((t, n), BF16),
            jax.ShapeDtypeStruct((hp, t), F32),
        ],
        scratch_shapes=[pltpu.VMEM((bm, d), BF16)],
        compiler_params=_cparams("parallel", "arbitrary"),
        name="ssd_in",
    )(x, g, w_main, w_dt_t, dt_bias_col)


def _ssd_kernel(z_ref, x_ref, b_ref, c_ref, dtt_ref, alog_ref, cwx_ref, cwb_ref, cwc_ref,
                cbx_ref, cbb_ref, cbc_ref, dsk_ref, ng_ref, y_ref, ubuf, state_scr, *, L, gw):
    c_idx = pl.program_id(2)
    hg = gw // SSD_HEADDIM
    n = SSD_STATE
    tail = 8

    @pl.when(c_idx == 0)
    def _():
        ubuf[0:tail, :] = jnp.zeros((tail, ubuf.shape[1]), F32)
        state_scr[...] = jnp.zeros(state_scr.shape, F32)

    ubuf[tail:tail + L, 0:gw] = x_ref[...].astype(F32)
    ubuf[tail:tail + L, gw:gw + n] = b_ref[...].astype(F32)
    ubuf[tail:tail + L, gw + n:gw + 2 * n] = c_ref[...].astype(F32)

    cw = jnp.concatenate([cwx_ref[...], cwb_ref[...], cwc_ref[...]], axis=1)
    cb = jnp.concatenate([cbx_ref[...], cbb_ref[...], cbc_ref[...]], axis=1)
    conv = cb + cw[SSD_CONV - 1:SSD_CONV, :] * ubuf[tail:tail + L, :]
    for kk in range(SSD_CONV - 1):
        off = tail - (SSD_CONV - 1) + kk
        conv = conv + cw[kk:kk + 1, :] * ubuf[off:off + L, :]
    ubuf[0:tail, :] = ubuf[L:L + tail, :]
    u = _silu(conv)
    xc = u[:, 0:gw]
    bc = u[:, gw:gw + n].astype(BF16)
    cc = u[:, gw + n:gw + 2 * n].astype(BF16)

    dtt = dtt_ref[...]
    a_t = dtt * (-LOG2_E * jnp.exp(alog_ref[...]))
    ri = lax.broadcasted_iota(I32, (L, L), 0)
    ci = lax.broadcasted_iota(I32, (L, L), 1)
    causal = ri >= ci
    tril = jnp.where(causal, 1.0, 0.0).astype(BF16)
    r3 = _dot_nt(_split3(a_t), tril)
    acum_row = r3[0:hg] + r3[hg:2 * hg] + r3[2 * hg:3 * hg]

    def expansion(width):
        hrow = lax.broadcasted_iota(I32, (4 * hg, hg * width), 0)
        hlane = lax.broadcasted_iota(I32, (4 * hg, hg * width), 1) // width
        return jnp.where((hrow % hg == hlane) & (hrow < 3 * hg), 1.0, 0.0).astype(BF16)

    acum3 = _split3(acum_row)
    dt_e = _dot_tn(_split3(dtt), expansion(SSD_HEADDIM))
    acum_e = _dot_tn(acum3, expansion(SSD_HEADDIM))
    acum_c = _dot_tn(acum3, expansion(LANES))
    last_e = acum_e[L - 1:L, :]

    xdt = xc * dt_e
    cbm = jnp.where(causal, _dot_nt(cc, bc), 0.0)

    st = state_scr[...]
    y = _dot(cc, st.astype(BF16)) * jnp.exp2(acum_e) + xc * dsk_ref[...]

    lane = lax.broadcasted_iota(I32, (L, LANES), 1)
    pieces = []
    for pair in range(hg // 2):
        xp = xdt[:, pair * LANES:(pair + 1) * LANES]
        yp = None
        for sub in range(2):
            jh = 2 * pair + sub
            col = _tile_lanes(acum_c[:, jh * LANES:(jh + 1) * LANES], L // LANES)
            seg = jnp.minimum(col - acum_row[jh:jh + 1, :], 0.0)
            m = (cbm * jnp.exp2(seg)).astype(BF16)
            keep = (lane // SSD_HEADDIM) == sub
            d = _dot(m, jnp.where(keep, xp, 0.0).astype(BF16))
            yp = d if yp is None else yp + d
        pieces.append(yp)
    y = y + jnp.concatenate(pieces, axis=1)

    xw = (xdt * jnp.exp2(last_e - acum_e)).astype(BF16)
    state_scr[...] = st * jnp.exp2(last_e) + _dot_tn(bc, xw)

    zf = z_ref[...].astype(F32)
    yg = y * _silu(zf)
    yg = yg * lax.rsqrt(jnp.mean(yg * yg, axis=-1, keepdims=True) + RMS_EPS)
    y_ref[...] = (yg * ng_ref[...]).astype(y_ref.dtype)


def _ssd_core(zx, dtt, a_log_col, conv_w, conv_b, d_exp, norm_g, bsz, s, inner):
    t = zx.shape[0]
    L = SSD_CHUNK
    assert s % L == 0
    nc = s // L
    gw = inner // SSD_GROUPS
    hg = gw // SSD_HEADDIM
    n = SSD_STATE
    gn = SSD_GROUPS * n
    xb0 = inner // gw
    bb0 = (2 * inner) // n
    cb0 = (2 * inner + gn) // n
    cwb0 = inner // n
    cwc0 = (inner + gn) // n
    rowi = lambda b, g, c: b * nc + c
    return pl.pallas_call(
        functools.partial(_ssd_kernel, L=L, gw=gw),
        grid=(bsz, SSD_GROUPS, nc),
        in_specs=[
            pl.BlockSpec((L, gw), lambda b, g, c: (rowi(b, g, c), g)),
            pl.BlockSpec((L, gw), lambda b, g, c: (rowi(b, g, c), xb0 + g)),
            pl.BlockSpec((L, n), lambda b, g, c: (rowi(b, g, c), bb0 + g)),
            pl.BlockSpec((L, n), lambda b, g, c: (rowi(b, g, c), cb0 + g)),
            pl.BlockSpec((hg, L), lambda b, g, c: (g, rowi(b, g, c))),
            pl.BlockSpec((hg, 1), lambda b, g, c: (g, 0)),
            pl.BlockSpec((SSD_CONV, gw), lambda b, g, c: (0, g)),
            pl.BlockSpec((SSD_CONV, n), lambda b, g, c: (0, cwb0 + g)),
            pl.BlockSpec((SSD_CONV, n), lambda b, g, c: (0, cwc0 + g)),
            pl.BlockSpec((1, gw), lambda b, g, c: (0, g)),
            pl.BlockSpec((1, n), lambda b, g, c: (0, cwb0 + g)),
            pl.BlockSpec((1, n), lambda b, g, c: (0, cwc0 + g)),
            pl.BlockSpec((1, gw), lambda b, g, c: (0, g)),
            pl.BlockSpec((1, gw), lambda b, g, c: (0, g)),
        ],
        out_specs=pl.BlockSpec((L, gw), lambda b, g, c: (rowi(b, g, c), g)),
        out_shape=jax.ShapeDtypeStruct((t, inner), BF16),
        scratch_shapes=[
            pltpu.VMEM((L + 8, gw + 2 * n), F32),
            pltpu.VMEM((n, gw), F32),
        ],
        compiler_params=_cparams("parallel", "parallel", "arbitrary"),
        name="ssd_core",
    )(zx, zx, zx, zx, dtt, a_log_col, conv_w, conv_w, conv_w, conv_b, conv_b, conv_b, d_exp, norm_g)


def _ssd_out_router_kernel(a_ref, w_ref, x_ref, g_ref, r_ref, xo_ref, hnp_ref, gate_ref, meta_ref, cnt_ref,
                           run_scr):
    bm, d = xo_ref.shape

    @pl.when(pl.program_id(0) == 0)
    def _():
        run_scr[...] = jnp.zeros(run_scr.shape, F32)

    xo = x_ref[...] + _dot(a_ref[...], w_ref[...])
    xo_ref[...] = xo

    hn = _rms(xo, g_ref[...])
    hn_hi = hn.astype(BF16)
    bits = pltpu.bitcast(hn_hi.astype(F32), U32)
    hnp_ref[...] = (bits[:, d // 2:] & jnp.uint32(0xFFFF0000)) | (bits[:, :d // 2] >> 16)

    hn_lo = (hn - hn_hi.astype(F32)).astype(BF16)
    rt = r_ref[...]
    rt_hi = rt.astype(BF16)
    rt_lo = (rt - rt_hi.astype(F32)).astype(BF16)
    logits = _dot(hn_hi, rt_hi) + (_dot(hn_lo, rt_hi) + _dot(hn_hi, rt_lo))
    lane = lax.broadcasted_iota(I32, logits.shape, 1)
    logits = jnp.where(lane < N_EXPERTS, logits, -jnp.inf)
    m1 = jnp.max(logits, axis=1, keepdims=True)
    i1 = jnp.min(jnp.where(logits == m1, lane, LANES), axis=1, keepdims=True)
    rest = jnp.where(lane == i1, -jnp.inf, logits)
    m2 = jnp.max(rest, axis=1, keepdims=True)
    i2 = jnp.min(jnp.where(rest == m2, lane, LANES), axis=1, keepdims=True)
    e21 = jnp.exp(m2 - m1)
    g1 = 1.0 / (1.0 + e21)
    g2 = e21 / (1.0 + e21)
    gate_ref[...] = jnp.where(lane == 0, g1, jnp.where(lane == 1, g2, 0.0))

    oh1 = lane == i1
    oh2 = lane == i2
    both = (oh1 | oh2).astype(BF16)
    ri = lax.broadcasted_iota(I32, (bm, bm), 0)
    ci = lax.broadcasted_iota(I32, (bm, bm), 1)
    before = _dot((ri > ci).astype(BF16), both) + run_scr[...]
    r1 = jnp.sum(jnp.where(oh1, before, 0.0), axis=1, keepdims=True)
    r2 = jnp.sum(jnp.where(oh2, before, 0.0), axis=1, keepdims=True)
    meta = jnp.where(lane == 0, i1.astype(F32),
                     jnp.where(lane == 1, i2.astype(F32),
                               jnp.where(lane == 2, r1, jnp.where(lane == 3, r2, 0.0))))
    meta_ref[...] = meta.astype(I32)
    run_scr[...] = run_scr[...] + jnp.sum(both.astype(F32), axis=0, keepdims=True)
    cnt_ref[...] = jnp.broadcast_to(run_scr[...], cnt_ref.shape).astype(I32)


def _ssd_out_router(a, w, x, g, router_p):
    t, kdim = a.shape
    d = w.shape[1]
    bm = _blk(t, 256)
    row = lambda i: (i, 0)
    fixed = lambda i: (0, 0)
    return pl.pallas_call(
        _ssd_out_router_kernel,
        grid=(t // bm,),
        in_specs=[
            pl.BlockSpec((bm, kdim), row),
            _resident((kdim, d), fixed),
            pl.BlockSpec((bm, d), row),
            _resident((1, d), fixed),
            _resident((d, LANES), fixed),
        ],
        out_specs=[
            pl.BlockSpec((bm, d), row),
            pl.BlockSpec((bm, d // 2), row),
            pl.BlockSpec((bm, LANES), row),
            pl.BlockSpec((bm, LANES), row),
            pl.BlockSpec((8, LANES), fixed),
        ],
        out_shape=[
            jax.ShapeDtypeStruct((t, d), F32),
            jax.ShapeDtypeStruct((t, d // 2), U32),
            jax.ShapeDtypeStruct((t, LANES), F32),
            jax.ShapeDtypeStruct((t, LANES), I32),
            jax.ShapeDtypeStruct((8, LANES), I32),
        ],
        scratch_shapes=[pltpu.VMEM((1, LANES), F32)],
        compiler_params=_cparams("arbitrary"),
        name="ssd_out_router",
    )(a, w, x, g, router_p)


def _dispatch_kernel(pe_ref, pos_ref, hn_ref, xb_ref, zero_scr, sem, zsem):
    bm = hn_ref.shape[0]
    rows = zero_scr.shape[0]

    @pl.when(pl.program_id(0) == 0)
    def _():
        zero_scr[...] = jnp.zeros(zero_scr.shape, zero_scr.dtype)

        def clear(row0):
            cp = pltpu.make_async_copy(zero_scr, xb_ref.at[pl.ds(pl.multiple_of(row0, rows), rows)], zsem)
            cp.start()
            cp.wait()

        for e in range(N_EXPERTS):
            prev_end = pe_ref[e - 1] if e else 0

            @pl.when(pe_ref[e] > prev_end)
            def _():
                clear(pe_ref[e] - rows)

        def clear_block(b, carry):
            clear(b * rows)
            return carry

        lax.fori_loop(pe_ref[N_EXPERTS - 1] // rows, xb_ref.shape[0] // rows, clear_block, 0)

    def row_copy(r, kk):
        dst = pos_ref[0, 0, TOP_K * r + kk]
        return pltpu.make_async_copy(hn_ref.at[pl.ds(r, 1)], xb_ref.at[pl.ds(dst, 1)], sem)

    def start(r, carry):
        for kk in range(TOP_K):
            row_copy(r, kk).start()
        return carry

    def wait(r, carry):
        for kk in range(TOP_K):
            row_copy(r, kk).wait()
        return carry

    lax.fori_loop(0, bm, start, 0, unroll=8)
    lax.fori_loop(0, bm, wait, 0, unroll=8)


def _dispatch(hnp, pos_blocks, pad_ends, cap, rows):
    t, w = hnp.shape
    nb, _, two_bm = pos_blocks.shape
    bm = two_bm // TOP_K
    grid_spec = pltpu.PrefetchScalarGridSpec(
        num_scalar_prefetch=1,
        grid=(nb,),
        in_specs=[
            pl.BlockSpec((1, 1, two_bm), lambda i, pe: (i, 0, 0), memory_space=pltpu.SMEM),
            pl.BlockSpec((bm, w), lambda i, pe: (i, 0)),
        ],
        out_specs=pl.BlockSpec(memory_space=pl.ANY),
        scratch_shapes=[pltpu.VMEM((rows, w), U32), pltpu.SemaphoreType.DMA(()), pltpu.SemaphoreType.DMA(())],
    )
    return pl.pallas_call(
        _dispatch_kernel,
        grid_spec=grid_spec,
        out_shape=jax.ShapeDtypeStruct((cap, w), U32),
        compiler_params=_cparams("arbitrary"),
        name="moe_dispatch",
    )(pad_ends, pos_blocks, hnp)


def _moe_kernel(be_ref, nu_ref, xb_ref, wg_ref, wu_ref, wd_ref, o_ref, xs_scr):
    del be_ref
    b = pl.program_id(0)
    f = pl.program_id(1)

    @pl.when(f == 0)
    def _():
        o_ref[...] = jnp.zeros(o_ref.shape, F32)

    @pl.when(b < nu_ref[0])
    def _():
        @pl.when(f == 0)
        def _():
            w = xb_ref[...]
            half = w.shape[1]
            xs_scr[:, :half] = pltpu.bitcast(w << 16, F32).astype(BF16)
            xs_scr[:, half:] = pltpu.bitcast(w & jnp.uint32(0xFFFF0000), F32).astype(BF16)

        xs = xs_scr[...]
        h = (_silu(_dot(xs, wg_ref[0])) * _dot(xs, wu_ref[0])).astype(BF16)
        o_ref[...] += _dot(h, wd_ref[0])


def _moe(xb, block_expert, n_used, wg, wu, wd, bm):
    cap, half = xb.shape
    d = 2 * half
    fdim = wg.shape[2]
    bf = _blk(fdim, 1024)
    nf = fdim // bf
    nb = cap // bm

    def live(b, nu):
        return jnp.maximum(jnp.minimum(b, nu[0] - 1), 0)

    def f_eff(b, f, nu):
        return jnp.where(b < nu[0], f, nf - 1)

    grid_spec = pltpu.PrefetchScalarGridSpec(
        num_scalar_prefetch=2,
        grid=(nb, nf),
        in_specs=[
            pl.BlockSpec((bm, half), lambda b, f, be, nu: (live(b, nu), 0)),
            pl.BlockSpec((1, d, bf), lambda b, f, be, nu: (be[live(b, nu)], 0, f_eff(b, f, nu))),
            pl.BlockSpec((1, d, bf), lambda b, f, be, nu: (be[live(b, nu)], 0, f_eff(b, f, nu))),
            pl.BlockSpec((1, bf, d), lambda b, f, be, nu: (be[live(b, nu)], f_eff(b, f, nu), 0)),
        ],
        out_specs=pl.BlockSpec((bm, d), lambda b, f, be, nu: (b, 0)),
        scratch_shapes=[pltpu.VMEM((bm, d), BF16)],
    )
    return pl.pallas_call(
        _moe_kernel,
        grid_spec=grid_spec,
        out_shape=jax.ShapeDtypeStruct((cap, d), F32),
        compiler_params=_cparams("arbitrary", "arbitrary"),
        name="moe_ffn",
    )(block_expert, n_used, xb, wg, wu, wd)


def _combine_kernel(pos_ref, x_ref, gate_ref, g_ref, yb_ref, o_ref, ybuf, sem):
    bm = x_ref.shape[0]

    def row_copy(r, kk):
        src = pos_ref[0, 0, TOP_K * r + kk]
        return pltpu.make_async_copy(yb_ref.at[pl.ds(src, 1)], ybuf.at[kk, pl.ds(r, 1)], sem)

    def start(r, carry):
        for kk in range(TOP_K):
            row_copy(r, kk).start()
        return carry

    def wait(r, carry):
        for kk in range(TOP_K):
            row_copy(r, kk).wait()
        return carry

    lax.fori_loop(0, bm, start, 0, unroll=8)
    lax.fori_loop(0, bm, wait, 0, unroll=8)
    gates = gate_ref[...]
    x = x_ref[...] + gates[:, 0:1] * ybuf[0] + gates[:, 1:2] * ybuf[1]
    o_ref[...] = _rms(x, g_ref[...])


def _combine(x, gates, g_final, yb, pos_blocks):
    t, d = x.shape
    nb, _, two_bm = pos_blocks.shape
    bm = two_bm // TOP_K
    return pl.pallas_call(
        _combine_kernel,
        grid=(nb,),
        in_specs=[
            pl.BlockSpec((1, 1, two_bm), lambda i: (i, 0, 0), memory_space=pltpu.SMEM),
            pl.BlockSpec((bm, d), lambda i: (i, 0)),
            pl.BlockSpec((bm, LANES), lambda i: (i, 0)),
            _resident((1, d), lambda i: (0, 0)),
            pl.BlockSpec(memory_space=pl.ANY),
        ],
        out_specs=pl.BlockSpec((bm, d), lambda i: (i, 0)),
        out_shape=jax.ShapeDtypeStruct((t, d), F32),
        scratch_shapes=[pltpu.VMEM((TOP_K, bm, d), F32), pltpu.SemaphoreType.DMA(())],
        compiler_params=_cparams("arbitrary"),
        name="moe_combine",
    )(pos_blocks, x, gates, g_final, yb)


def _rope_tables(positions):
    inv_freq = ROPE_THETA ** (-jnp.arange(0, MLA_ROPE, 2, dtype=F32) / MLA_ROPE)
    ang = positions.astype(F32).reshape(-1)[:, None] * inv_freq
    cos, sin = jnp.cos(ang), jnp.sin(ang)
    zeros = jnp.zeros((ang.shape[0], LANES - MLA_ROPE), F32)
    return jnp.concatenate([cos, cos, zeros], axis=1), jnp.concatenate([-sin, sin, zeros], axis=1)


def _swap_halves(w):
    half = w.shape[-1] // 2
    return jnp.concatenate([w[..., half:], w[..., :half]], axis=-1)


def _mla_weights(w_in, w_uq, w_ukv):
    d = w_in.shape[0]
    o = MLA_Q_RANK + MLA_KV_RANK
    w_kr = w_in[:, o:]
    zpad = jnp.zeros((d, LANES - MLA_ROPE), w_in.dtype)
    w_in_p = jnp.concatenate([w_in[:, :o], w_kr, zpad, _swap_halves(w_kr), zpad], axis=1).astype(BF16)
    uq = w_uq.reshape(MLA_Q_RANK, MLA_HEADS, MLA_NOPE + MLA_ROPE)
    rope = uq[:, :, MLA_NOPE:]
    zq = jnp.zeros((MLA_Q_RANK, MLA_HEADS, LANES - MLA_ROPE), w_uq.dtype)
    wq = jnp.concatenate([uq[:, :, :MLA_NOPE], rope, zq], axis=2).reshape(MLA_Q_RANK, -1).astype(BF16)
    wqs = jnp.concatenate([_swap_halves(rope), zq], axis=2).reshape(MLA_Q_RANK, -1).astype(BF16)
    ukv = w_ukv.reshape(MLA_KV_RANK, MLA_HEADS, MLA_NOPE + MLA_V)
    wk = ukv[:, :, :MLA_NOPE].reshape(MLA_KV_RANK, -1).astype(BF16)
    wv = ukv[:, :, MLA_NOPE:].reshape(MLA_KV_RANK, -1).astype(BF16)
    return w_in_p, wq, wqs, wk, wv


def kernel(x, positions, mla_w_in, mla_q_norm, mla_kv_norm, mla_w_uq, mla_w_ukv, mla_w_o, ssd_w_in, ssd_conv_w, ssd_conv_b, ssd_dt_bias, ssd_a_log, ssd_d, ssd_norm, ssd_w_o, ffn_w_gate, ffn_w_up, ffn_w_down, moe_router, moe_w_gate, moe_w_up, moe_w_down, norm_mix, norm_ffn, norm_final):
    bsz, s, d = x.shape
    t = bsz * s
    assert norm_mix.shape[0] == 2 and mla_w_in.shape[0] == 1 and ssd_w_in.shape[0] == 1
    xf = x.reshape(t, d)
    cr, sr = _rope_tables(positions)
    row = lambda v: v.reshape(1, -1)
    bf = lambda v: v.astype(BF16)

    w_in_p, wq, wqs, wk, wv = _mla_weights(mla_w_in[0], mla_w_uq[0], mla_w_ukv[0])
    cq, ckv, kr = _mla_in(xf, row(norm_mix[0]), w_in_p, row(mla_q_norm[0]), row(mla_kv_norm[0]), cr, sr)
    q, k, v = _mla_up(cq, ckv, kr, wq, wqs, wk, wv, cr, sr)
    inner = ssd_norm.shape[1]
    heads = inner // SSD_HEADDIM
    n_main = 2 * inner + 2 * SSD_GROUPS * SSD_STATE
    w_in = ssd_w_in[0]
    fdim = ffn_w_gate.shape[2]
    flat = lambda w: w.reshape(-1, w.shape[-1])
    full = lambda w: (flat(w), w.shape[-1])
    casts = [full(mla_w_o[0]), full(ffn_w_gate[0]), full(ffn_w_up[0]), full(ffn_w_down[0]),
             (w_in, n_main), full(ssd_w_o[0]), full(moe_w_gate[0]), full(moe_w_up[0]), full(moe_w_down[0])]
    o, (w_o_b, fg_b, fu_b, fd_b, ssd_in_b, ssd_o_b, mg_b, mu_b, md_b) = _flash(
        q.reshape(bsz, s, -1), k.reshape(bsz, s, -1), v.reshape(bsz, s, -1), bsz, s, casts)
    xf, hn = _proj_res_norm(o.reshape(t, -1), w_o_b, xf, row(norm_ffn[0]))
    xf = _ffn(hn, xf, fg_b, fu_b, fd_b)

    w_dt_t = bf(jnp.zeros((LANES, d), F32).at[:heads].set(w_in[:, n_main:].T))
    dtb = jnp.zeros((LANES, 1), F32).at[:heads, 0].set(ssd_dt_bias[0])
    zx, dtt = _ssd_in(xf, row(norm_mix[1]), ssd_in_b, w_dt_t, dtb)
    a_log_col = jnp.zeros((LANES, 1), F32).at[:heads, 0].set(ssd_a_log[0])
    d_exp = jnp.repeat(ssd_d[0], SSD_HEADDIM).reshape(1, inner)
    y = _ssd_core(zx, dtt, a_log_col, ssd_conv_w[0], row(ssd_conv_b[0]), d_exp, row(ssd_norm[0]), bsz, s, inner)
    router_p = jnp.zeros((d, LANES), F32).at[:, :N_EXPERTS].set(moe_router[0])
    xf, hnp, gates, meta, cnt = _ssd_out_router(y, ssd_o_b, xf, row(norm_ffn[1]), router_p)

    n_blocks = -(-(t * TOP_K) // MOE_ROWS) + N_EXPERTS
    counts = cnt[0, :N_EXPERTS]
    padded = ((counts + MOE_ROWS - 1) // MOE_ROWS) * MOE_ROWS
    pad_ends = jnp.cumsum(padded)
    pad_starts = pad_ends - padded
    pos = pad_starts[meta[:, 0:TOP_K]] + meta[:, TOP_K:2 * TOP_K]
    n_used = (pad_ends[-1] // MOE_ROWS).astype(I32).reshape(1)
    block_start = jnp.arange(n_blocks, dtype=I32)[:, None] * MOE_ROWS
    block_expert = jnp.minimum(jnp.sum(block_start >= pad_ends[None, :], axis=1), N_EXPERTS - 1).astype(I32)
    bm_d = _blk(t, 256)
    pos_blocks = pos.astype(I32).reshape(t // bm_d, 1, TOP_K * bm_d)
    xb = _dispatch(hnp, pos_blocks, pad_ends.astype(I32), n_blocks * MOE_ROWS, MOE_ROWS)
    yb = _moe(xb, block_expert, n_used, mg_b.reshape(N_EXPERTS, d, fdim), mu_b.reshape(N_EXPERTS, d, fdim),
              md_b.reshape(N_EXPERTS, fdim, d), MOE_ROWS)
    out = _combine(xf, gates, row(norm_final), yb, pos_blocks)
    return out.reshape(bsz, s, d)
```

```python
import functools

import jax
import jax.numpy as jnp
from jax import lax
from jax.experimental import pallas as pl
from jax.experimental.pallas import tpu as pltpu

F32 = jnp.float32
BF16 = jnp.bfloat16
U32 = jnp.uint32
I32 = jnp.int32

RMS_EPS = 1e-6
MLA_HEADS = 16
MLA_Q_RANK = 512
MLA_KV_RANK = 512
MLA_NOPE = 128
MLA_ROPE = 64
MLA_V = 128
ROPE_THETA = 10000.0
SSD_HEADDIM = 64
SSD_GROUPS = 8
SSD_STATE = 128
SSD_CONV = 4
SSD_CHUNK = 256
SSD_SLAB = 128
N_EXPERTS = 8
TOP_K = 2
MOE_ROWS = 512
FLASH_TQ = 1024
FLASH_ROWS = 512

LANES = 128
BF16_SUBLANES = 16
QK_WIDTH = 2 * LANES
VMEM_LIMIT_BYTES = 56 * 1024 * 1024
LOG2_E = 1.4426950408889634


def _cparams(*sem):
    return pltpu.CompilerParams(dimension_semantics=sem, vmem_limit_bytes=VMEM_LIMIT_BYTES)


def _blk(n, pref):
    b = min(n, pref)
    assert n % b == 0, (n, pref)
    return b


def _resident(shape, index_map):
    return pl.BlockSpec(shape, index_map, pipeline_mode=pl.Buffered(1))


def _rms(xf, g):
    r = lax.rsqrt(jnp.mean(xf * xf, axis=-1, keepdims=True) + RMS_EPS)
    return xf * r * g


def _silu(x):
    h = 0.5 * x
    return h + h * jnp.tanh(h)


def _dot(a, b):
    return jnp.dot(a, b, preferred_element_type=F32)


def _dot_nt(a, b, precision=None):
    return lax.dot_general(a, b, (((1,), (1,)), ((), ())), preferred_element_type=F32, precision=precision)


def _dot_tn(a, b):
    return lax.dot_general(a, b, (((0,), (0,)), ((), ())), preferred_element_type=F32)


def _tile_lanes(x, reps):
    return x if reps == 1 else jnp.concatenate([x] * reps, axis=1)


def _split3(x):
    hi = x.astype(BF16).astype(F32)
    r1 = x - hi
    mid = r1.astype(BF16).astype(F32)
    lo = r1 - mid
    return jnp.concatenate([hi, mid, lo, jnp.zeros_like(x)], axis=0).astype(BF16)


def _mla_in_kernel(x_ref, g_ref, w_ref, qn_ref, kvn_ref, cr_ref, sr_ref, cq_ref, ckv_ref, kr_ref):
    hn = _rms(x_ref[...], g_ref[...]).astype(BF16)
    c = _dot(hn, w_ref[...])
    cq_ref[...] = _rms(c[:, :MLA_Q_RANK], qn_ref[...]).astype(BF16)
    ckv_ref[...] = _rms(c[:, MLA_Q_RANK:MLA_Q_RANK + MLA_KV_RANK], kvn_ref[...]).astype(BF16)
    o = MLA_Q_RANK + MLA_KV_RANK
    kr_ref[...] = (c[:, o:o + LANES] * cr_ref[...] + c[:, o + LANES:o + 2 * LANES] * sr_ref[...]).astype(BF16)


def _mla_in(x, g, w_in_p, q_norm, kv_norm, cr, sr):
    t, d = x.shape
    bm = _blk(t, 512)
    wn = w_in_p.shape[1]
    row = lambda i: (i, 0)
    fixed = lambda i: (0, 0)
    return pl.pallas_call(
        _mla_in_kernel,
        grid=(t // bm,),
        in_specs=[
            pl.BlockSpec((bm, d), row),
            _resident((1, d), fixed),
            _resident((d, wn), fixed),
            _resident((1, MLA_Q_RANK), fixed),
            _resident((1, MLA_KV_RANK), fixed),
            pl.BlockSpec((bm, LANES), row),
            pl.BlockSpec((bm, LANES), row),
        ],
        out_specs=[
            pl.BlockSpec((bm, MLA_Q_RANK), row),
            pl.BlockSpec((bm, MLA_KV_RANK), row),
            pl.BlockSpec((bm, LANES), row),
        ],
        out_shape=[
            jax.ShapeDtypeStruct((t, MLA_Q_RANK), BF16),
            jax.ShapeDtypeStruct((t, MLA_KV_RANK), BF16),
            jax.ShapeDtypeStruct((t, LANES), BF16),
        ],
        compiler_params=_cparams("parallel"),
        name="mla_in",
    )(x, g, w_in_p, q_norm, kv_norm, cr, sr)


def _mla_up_kernel(cq_ref, ckv_ref, kr_ref, wq_ref, wqs_ref, wk_ref, wv_ref, cr_ref, sr_ref,
                   q_ref, k_ref, v_ref, *, scale):
    cq = cq_ref[...]
    ckv = ckv_ref[...]
    a = _dot(cq, wq_ref[...])
    b = _dot(cq, wqs_ref[...])
    kn = _dot(ckv, wk_ref[...])
    v_ref[...] = _dot(ckv, wv_ref[...]).astype(BF16)
    cr = cr_ref[...] * scale
    sr = sr_ref[...] * scale
    kr = kr_ref[...]
    for h in range(MLA_HEADS):
        q0 = h * QK_WIDTH
        q_ref[:, q0:q0 + LANES] = (a[:, q0:q0 + LANES] * scale).astype(BF16)
        q_ref[:, q0 + LANES:q0 + QK_WIDTH] = (
            a[:, q0 + LANES:q0 + QK_WIDTH] * cr + b[:, h * LANES:(h + 1) * LANES] * sr).astype(BF16)
        k_ref[:, q0:q0 + LANES] = kn[:, h * LANES:(h + 1) * LANES].astype(BF16)
        k_ref[:, q0 + LANES:q0 + QK_WIDTH] = kr


def _mla_up(cq, ckv, kr, wq, wqs, wk, wv, cr, sr):
    t = cq.shape[0]
    bm = _blk(t, 512)
    row = lambda i: (i, 0)
    fixed = lambda i: (0, 0)
    hq = MLA_HEADS * QK_WIDTH
    hv = MLA_HEADS * MLA_V
    scale = float((MLA_NOPE + MLA_ROPE) ** -0.5) * LOG2_E
    return pl.pallas_call(
        functools.partial(_mla_up_kernel, scale=scale),
        grid=(t // bm,),
        in_specs=[
            pl.BlockSpec((bm, MLA_Q_RANK), row),
            pl.BlockSpec((bm, MLA_KV_RANK), row),
            pl.BlockSpec((bm, LANES), row),
            _resident(wq.shape, fixed),
            _resident(wqs.shape, fixed),
            _resident(wk.shape, fixed),
            _resident(wv.shape, fixed),
            pl.BlockSpec((bm, LANES), row),
            pl.BlockSpec((bm, LANES), row),
        ],
        out_specs=[
            pl.BlockSpec((bm, hq), row),
            pl.BlockSpec((bm, hq), row),
            pl.BlockSpec((bm, hv), row),
        ],
        out_shape=[
            jax.ShapeDtypeStruct((t, hq), BF16),
            jax.ShapeDtypeStruct((t, hq), BF16),
            jax.ShapeDtypeStruct((t, hv), BF16),
        ],
        compiler_params=_cparams("parallel"),
        name="mla_up",
    )(cq, ckv, kr, wq, wqs, wk, wv, cr, sr)


def _flash_kernel(q_ref, k_ref, v_ref, *rest, tq, rows, ncast):
    w_refs = rest[:ncast]
    o_ref = rest[ncast]
    wo_refs = rest[ncast + 1:2 * ncast + 1]
    m_scr, l_scr, acc_scr = rest[2 * ncast + 1:]

    i = pl.program_id(2)
    m_scr[...] = jnp.full(m_scr.shape, -jnp.inf, F32)
    l_scr[...] = jnp.zeros(l_scr.shape, F32)
    acc_scr[...] = jnp.zeros(acc_scr.shape, F32)

    def update(row0, nrows, k0, width, masked):
        rs = slice(row0, row0 + nrows)
        s = _dot_nt(q_ref[0, rs, :], k_ref[0, pl.ds(k0, width), :])
        if masked:
            qrel = row0 + lax.broadcasted_iota(I32, (nrows, width), 0)
            krel = lax.broadcasted_iota(I32, (nrows, width), 1)
            s = jnp.where(krel <= qrel, s, -jnp.inf)
        m_prev = m_scr[rs, :]
        m_new = jnp.maximum(m_prev, jnp.max(s, axis=1, keepdims=True))
        alpha = jnp.exp2(m_prev - m_new)
        p = jnp.exp2(s - _tile_lanes(m_new, width // LANES))
        l_scr[rs, :] = alpha * l_scr[rs, :] + jnp.sum(p, axis=1, keepdims=True)
        acc_scr[rs, :] = alpha * acc_scr[rs, :] + _dot(p.astype(BF16), v_ref[0, pl.ds(k0, width), :])
        m_scr[rs, :] = m_new

    def body(j, carry):
        update(0, tq, pl.multiple_of(j * tq, tq), tq, False)
        return carry

    lax.fori_loop(0, i, body, 0)
    for w_ref, wo_ref in zip(w_refs, wo_refs):
        wo_ref[...] = w_ref[...].astype(BF16)
    for r in range(tq // rows):
        update(r * rows, rows, pl.multiple_of(i * tq, tq), (r + 1) * rows, True)
    o_ref[0] = (acc_scr[...] / l_scr[...]).astype(o_ref.dtype)


def _cast_rows(nrows, nsteps):
    for br in range(BF16_SUBLANES, nrows + 1, BF16_SUBLANES):
        if nrows % br == 0 and (nsteps * br) % nrows == 0 and nsteps * br >= nrows:
            return br
    raise ValueError((nrows, nsteps))


def _flash(q, k, v, bsz, s, casts):
    assert MLA_V == LANES
    tq = _blk(s, FLASH_TQ)
    rows = _blk(tq, FLASH_ROWS)
    nq = s // tq
    nsteps = bsz * MLA_HEADS * nq
    cast_specs, cast_shapes = [], []
    for w in casts:
        nrows, cols = w.shape
        br = _cast_rows(nrows, nsteps)
        per_block = nsteps * br // nrows
        spec = pl.BlockSpec((br, cols), lambda b, h, i, pb=per_block: (((b * MLA_HEADS + h) * nq + i) // pb, 0))
        cast_specs.append(spec)
        cast_shapes.append(jax.ShapeDtypeStruct((nrows, cols), BF16))
    outs = pl.pallas_call(
        functools.partial(_flash_kernel, tq=tq, rows=rows, ncast=len(casts)),
        grid=(bsz, MLA_HEADS, nq),
        in_specs=[
            pl.BlockSpec((1, tq, QK_WIDTH), lambda b, h, i: (b, i, h)),
            pl.BlockSpec((1, s, QK_WIDTH), lambda b, h, i: (b, 0, h)),
            pl.BlockSpec((1, s, MLA_V), lambda b, h, i: (b, 0, h)),
        ] + cast_specs,
        out_specs=[pl.BlockSpec((1, tq, MLA_V), lambda b, h, i: (b, i, h))] + cast_specs,
        out_shape=[jax.ShapeDtypeStruct((bsz, s, MLA_HEADS * MLA_V), BF16)] + cast_shapes,
        scratch_shapes=[
            pltpu.VMEM((tq, LANES), F32),
            pltpu.VMEM((tq, LANES), F32),
            pltpu.VMEM((tq, MLA_V), F32),
        ],
        compiler_params=_cparams("arbitrary", "arbitrary", "arbitrary"),
        name="mla_flash",
    )(q, k, v, *casts)
    return outs[0], outs[1:]


def _proj_res_norm_kernel(a_ref, w_ref, x_ref, g_ref, xo_ref, hn_ref):
    xo = x_ref[...] + _dot(a_ref[...], w_ref[...])
    xo_ref[...] = xo
    hn_ref[...] = _rms(xo, g_ref[...]).astype(hn_ref.dtype)


def _proj_res_norm(a, w, x, g):
    t, kdim = a.shape
    d = w.shape[1]
    bm = _blk(t, 512)
    row = lambda i: (i, 0)
    fixed = lambda i: (0, 0)
    return pl.pallas_call(
        _proj_res_norm_kernel,
        grid=(t // bm,),
        in_specs=[
            pl.BlockSpec((bm, kdim), row),
            _resident((kdim, d), fixed),
            pl.BlockSpec((bm, d), row),
            _resident((1, d), fixed),
        ],
        out_specs=[pl.BlockSpec((bm, d), row), pl.BlockSpec((bm, d), row)],
        out_shape=[
            jax.ShapeDtypeStruct((t, d), F32),
            jax.ShapeDtypeStruct((t, d), BF16),
        ],
        compiler_params=_cparams("parallel"),
        name="proj_res_norm",
    )(a, w, x, g)


def _ffn_kernel(hn_ref, x_ref, wg_ref, wu_ref, wd_ref, o_ref):
    f = pl.program_id(1)

    @pl.when(f == 0)
    def _():
        o_ref[...] = x_ref[...]

    hn = hn_ref[...]
    h = (_silu(_dot(hn, wg_ref[...])) * _dot(hn, wu_ref[...])).astype(BF16)
    o_ref[...] += _dot(h, wd_ref[...])


def _ffn(hn, x, wg, wu, wd):
    t, d = x.shape
    fdim = wg.shape[1]
    bm = _blk(t, 512)
    bf = _blk(fdim, 1024)
    return pl.pallas_call(
        _ffn_kernel,
        grid=(t // bm, fdim // bf),
        in_specs=[
            pl.BlockSpec((bm, d), lambda i, f: (i, 0)),
            pl.BlockSpec((bm, d), lambda i, f: (i, 0)),
            pl.BlockSpec((d, bf), lambda i, f: (0, f)),
            pl.BlockSpec((d, bf), lambda i, f: (0, f)),
            pl.BlockSpec((bf, d), lambda i, f: (f, 0)),
        ],
        out_specs=pl.BlockSpec((bm, d), lambda i, f: (i, 0)),
        out_shape=jax.ShapeDtypeStruct((t, d), F32),
        compiler_params=_cparams("parallel", "arbitrary"),
        name="dense_ffn",
    )(hn, x, wg, wu, wd)


def _ssd_in_kernel(x_ref, g_ref, w_ref, wdt_ref, dtb_ref, zx_ref, dtt_ref, hn_scr):
    j = pl.program_id(1)

    @pl.when(j == 0)
    def _():
        hn = _rms(x_ref[...], g_ref[...]).astype(BF16)
        hn_scr[...] = hn
        raw = _dot_nt(wdt_ref[...], hn) + dtb_ref[...]
        dtt_ref[...] = jnp.maximum(raw, 0.0) + jnp.log1p(jnp.exp(-jnp.abs(raw)))

    zx_ref[...] = _dot(hn_scr[...], w_ref[...]).astype(zx_ref.dtype)


def _ssd_in(x, g, w_main, w_dt_t, dt_bias_col):
    t, d = x.shape
    n = w_main.shape[1]
    bm = _blk(t, 1024)
    bn = _blk(n, 1024)
    hp = w_dt_t.shape[0]
    return pl.pallas_call(
        _ssd_in_kernel,
        grid=(t // bm, n // bn),
        in_specs=[
            pl.BlockSpec((bm, d), lambda i, j: (i, 0)),
            _resident((1, d), lambda i, j: (0, 0)),
            pl.BlockSpec((d, bn), lambda i, j: (0, j)),
            _resident((hp, d), lambda i, j: (0, 0)),
            _resident((hp, 1), lambda i, j: (0, 0)),
        ],
        out_specs=[
            pl.BlockSpec((bm, bn), lambda i, j: (i, j)),
            pl.BlockSpec((hp, bm), lambda i, j: (0, i)),
        ],
        out_shape=[
            jax.ShapeDtypeStruct((t, n), BF16),
            jax.ShapeDtypeStruct((hp, t), F32),
        ],
        scratch_shapes=[pltpu.VMEM((bm, d), BF16)],
        compiler_params=_cparams("parallel", "arbitrary"),
        name="ssd_in",
    )(x, g, w_main, w_dt_t, dt_bias_col)


def _ssd_kernel(z_ref, x_ref, b_ref, c_ref, dtt_ref, alog_ref, cwx_ref, cwb_ref, cwc_ref,
                cbx_ref, cbb_ref, cbc_ref, dsk_ref, ng_ref, y_ref, ubuf, state_scr, *, L, gw):
    c_idx = pl.program_id(2)
    hg = gw // SSD_HEADDIM
    n = SSD_STATE
    tail = 8

    @pl.when(c_idx == 0)
    def _():
        ubuf[0:tail, :] = jnp.zeros((tail, ubuf.shape[1]), F32)
        state_scr[...] = jnp.zeros(state_scr.shape, F32)

    ubuf[tail:tail + L, 0:gw] = x_ref[...].astype(F32)
    ubuf[tail:tail + L, gw:gw + n] = b_ref[...].astype(F32)
    ubuf[tail:tail + L, gw + n:gw + 2 * n] = c_ref[...].astype(F32)

    cw = jnp.concatenate([cwx_ref[...], cwb_ref[...], cwc_ref[...]], axis=1)
    cb = jnp.concatenate([cbx_ref[...], cbb_ref[...], cbc_ref[...]], axis=1)
    conv = cb + cw[SSD_CONV - 1:SSD_CONV, :] * ubuf[tail:tail + L, :]
    for kk in range(SSD_CONV - 1):
        off = tail - (SSD_CONV - 1) + kk
        conv = conv + cw[kk:kk + 1, :] * ubuf[off:off + L, :]
    ubuf[0:tail, :] = ubuf[L:L + tail, :]
    u = _silu(conv)
    xc = u[:, 0:gw]
    bc = u[:, gw:gw + n].astype(BF16)
    cc = u[:, gw + n:gw + 2 * n].astype(BF16)

    dtt = dtt_ref[...]
    a_t = dtt * (-LOG2_E * jnp.exp(alog_ref[...]))
    ri = lax.broadcasted_iota(I32, (L, L), 0)
    ci = lax.broadcasted_iota(I32, (L, L), 1)
    causal = ri >= ci
    tril = jnp.where(causal, 1.0, 0.0).astype(BF16)
    r3 = _dot_nt(_split3(a_t), tril)
    acum_row = r3[0:hg] + r3[hg:2 * hg] + r3[2 * hg:3 * hg]

    def expansion(width):
        hrow = lax.broadcasted_iota(I32, (4 * hg, hg * width), 0)
        hlane = lax.broadcasted_iota(I32, (4 * hg, hg * width), 1) // width
        return jnp.where((hrow % hg == hlane) & (hrow < 3 * hg), 1.0, 0.0).astype(BF16)

    acum3 = _split3(acum_row)
    dt_e = _dot_tn(_split3(dtt), expansion(SSD_HEADDIM))
    acum_e = _dot_tn(acum3, expansion(SSD_HEADDIM))
    acum_c = _dot_tn(acum3, expansion(LANES))
    last_e = acum_e[L - 1:L, :]

    xdt = xc * dt_e
    st = state_scr[...]
    st_b = st.astype(BF16)
    xw = (xdt * jnp.exp2(last_e - acum_e)).astype(BF16)
    state_scr[...] = st * jnp.exp2(last_e) + _dot_tn(bc, xw)

    lane = lax.broadcasted_iota(I32, (L, LANES), 1)
    xhead = []
    for jh in range(hg):
        xp = xdt[:, (jh // 2) * LANES:(jh // 2 + 1) * LANES]
        xhead.append(jnp.where((lane // SSD_HEADDIM) == jh % 2, xp, 0.0).astype(BF16))

    dsk = dsk_ref[...]
    ng = ng_ref[...]
    for r0 in range(0, L, SSD_SLAB):
        rs = slice(r0, r0 + SSD_SLAB)
        cbm = jnp.where(causal[rs, :], _dot_nt(cc[rs, :], bc), 0.0)
        y = _dot(cc[rs, :], st_b) * jnp.exp2(acum_e[rs, :]) + xc[rs, :] * dsk
        pieces = []
        for pair in range(hg // 2):
            yp = None
            for sub in range(2):
                jh = 2 * pair + sub
                col = _tile_lanes(acum_c[rs, jh * LANES:(jh + 1) * LANES], L // LANES)
                seg = jnp.minimum(col - acum_row[jh:jh + 1, :], 0.0)
                d = _dot((cbm * jnp.exp2(seg)).astype(BF16), xhead[jh])
                yp = d if yp is None else yp + d
            pieces.append(yp)
        y = y + jnp.concatenate(pieces, axis=1)
        yg = y * _silu(z_ref[rs, :].astype(F32))
        yg = yg * lax.rsqrt(jnp.mean(yg * yg, axis=-1, keepdims=True) + RMS_EPS)
        y_ref[rs, :] = (yg * ng).astype(y_ref.dtype)


def _ssd_core(zx, dtt, a_log_col, conv_w, conv_b, d_exp, norm_g, bsz, s, inner):
    t = zx.shape[0]
    L = SSD_CHUNK
    assert s % L == 0
    nc = s // L
    gw = inner // SSD_GROUPS
    hg = gw // SSD_HEADDIM
    n = SSD_STATE
    gn = SSD_GROUPS * n
    xb0 = inner // gw
    bb0 = (2 * inner) // n
    cb0 = (2 * inner + gn) // n
    cwb0 = inner // n
    cwc0 = (inner + gn) // n
    rowi = lambda b, g, c: b * nc + c
    return pl.pallas_call(
        functools.partial(_ssd_kernel, L=L, gw=gw),
        grid=(bsz, SSD_GROUPS, nc),
        in_specs=[
            pl.BlockSpec((L, gw), lambda b, g, c: (rowi(b, g, c), g)),
            pl.BlockSpec((L, gw), lambda b, g, c: (rowi(b, g, c), xb0 + g)),
            pl.BlockSpec((L, n), lambda b, g, c: (rowi(b, g, c), bb0 + g)),
            pl.BlockSpec((L, n), lambda b, g, c: (rowi(b, g, c), cb0 + g)),
            pl.BlockSpec((hg, L), lambda b, g, c: (g, rowi(b, g, c))),
            pl.BlockSpec((hg, 1), lambda b, g, c: (g, 0)),
            pl.BlockSpec((SSD_CONV, gw), lambda b, g, c: (0, g)),
            pl.BlockSpec((SSD_CONV, n), lambda b, g, c: (0, cwb0 + g)),
            pl.BlockSpec((SSD_CONV, n), lambda b, g, c: (0, cwc0 + g)),
            pl.BlockSpec((1, gw), lambda b, g, c: (0, g)),
            pl.BlockSpec((1, n), lambda b, g, c: (0, cwb0 + g)),
            pl.BlockSpec((1, n), lambda b, g, c: (0, cwc0 + g)),
            pl.BlockSpec((1, gw), lambda b, g, c: (0, g)),
            pl.BlockSpec((1, gw), lambda b, g, c: (0, g)),
        ],
        out_specs=pl.BlockSpec((L, gw), lambda b, g, c: (rowi(b, g, c), g)),
        out_shape=jax.ShapeDtypeStruct((t, inner), BF16),
        scratch_shapes=[
            pltpu.VMEM((L + 8, gw + 2 * n), F32),
            pltpu.VMEM((n, gw), F32),
        ],
        compiler_params=_cparams("parallel", "parallel", "arbitrary"),
        name="ssd_core",
    )(zx, zx, zx, zx, dtt, a_log_col, conv_w, conv_w, conv_w, conv_b, conv_b, conv_b, d_exp, norm_g)


def _ssd_out_router_kernel(a_ref, w_ref, x_ref, g_ref, r_ref, xo_ref, hnp_ref, gate_ref, meta_ref, cnt_ref,
                           run_scr):
    bm, d = xo_ref.shape

    @pl.when(pl.program_id(0) == 0)
    def _():
        run_scr[...] = jnp.zeros(run_scr.shape, F32)

    xo = x_ref[...] + _dot(a_ref[...], w_ref[...])
    xo_ref[...] = xo

    hn = _rms(xo, g_ref[...])
    hn_hi = hn.astype(BF16)
    bits = pltpu.bitcast(hn_hi.astype(F32), U32)
    hnp_ref[...] = (bits[:, d // 2:] & jnp.uint32(0xFFFF0000)) | (bits[:, :d // 2] >> 16)

    hn_lo = (hn - hn_hi.astype(F32)).astype(BF16)
    rt = r_ref[...]
    rt_hi = rt.astype(BF16)
    rt_lo = (rt - rt_hi.astype(F32)).astype(BF16)
    logits = _dot(hn_hi, rt_hi) + (_dot(hn_lo, rt_hi) + _dot(hn_hi, rt_lo))
    lane = lax.broadcasted_iota(I32, logits.shape, 1)
    logits = jnp.where(lane < N_EXPERTS, logits, -jnp.inf)
    m1 = jnp.max(logits, axis=1, keepdims=True)
    i1 = jnp.min(jnp.where(logits == m1, lane, LANES), axis=1, keepdims=True)
    rest = jnp.where(lane == i1, -jnp.inf, logits)
    m2 = jnp.max(rest, axis=1, keepdims=True)
    i2 = jnp.min(jnp.where(rest == m2, lane, LANES), axis=1, keepdims=True)
    e21 = jnp.exp(m2 - m1)
    g1 = 1.0 / (1.0 + e21)
    g2 = e21 / (1.0 + e21)
    gate_ref[...] = jnp.where(lane == 0, g1, jnp.where(lane == 1, g2, 0.0))

    oh1 = lane == i1
    oh2 = lane == i2
    both = (oh1 | oh2).astype(BF16)
    ri = lax.broadcasted_iota(I32, (bm, bm), 0)
    ci = lax.broadcasted_iota(I32, (bm, bm), 1)
    before = _dot((ri > ci).astype(BF16), both) + run_scr[...]
    r1 = jnp.sum(jnp.where(oh1, before, 0.0), axis=1, keepdims=True)
    r2 = jnp.sum(jnp.where(oh2, before, 0.0), axis=1, keepdims=True)
    meta = jnp.where(lane == 0, i1.astype(F32),
                     jnp.where(lane == 1, i2.astype(F32),
                               jnp.where(lane == 2, r1, jnp.where(lane == 3, r2, 0.0))))
    meta_ref[...] = meta.astype(I32)
    run_scr[...] = run_scr[...] + jnp.sum(both.astype(F32), axis=0, keepdims=True)
    cnt_ref[...] = jnp.broadcast_to(run_scr[...], cnt_ref.shape).astype(I32)


def _ssd_out_router(a, w, x, g, router_p):
    t, kdim = a.shape
    d = w.shape[1]
    bm = _blk(t, 256)
    row = lambda i: (i, 0)
    fixed = lambda i: (0, 0)
    return pl.pallas_call(
        _ssd_out_router_kernel,
        grid=(t // bm,),
        in_specs=[
            pl.BlockSpec((bm, kdim), row),
            _resident((kdim, d), fixed),
            pl.BlockSpec((bm, d), row),
            _resident((1, d), fixed),
            _resident((d, LANES), fixed),
        ],
        out_specs=[
            pl.BlockSpec((bm, d), row),
            pl.BlockSpec((bm, d // 2), row),
            pl.BlockSpec((bm, LANES), row),
            pl.BlockSpec((bm, LANES), row),
            pl.BlockSpec((8, LANES), fixed),
        ],
        out_shape=[
            jax.ShapeDtypeStruct((t, d), F32),
            jax.ShapeDtypeStruct((t, d // 2), U32),
            jax.ShapeDtypeStruct((t, LANES), F32),
            jax.ShapeDtypeStruct((t, LANES), I32),
            jax.ShapeDtypeStruct((8, LANES), I32),
        ],
        scratch_shapes=[pltpu.VMEM((1, LANES), F32)],
        compiler_params=_cparams("arbitrary"),
        name="ssd_out_router",
    )(a, w, x, g, router_p)


def _dispatch_kernel(pe_ref, pos_ref, hn_ref, xb_ref, zero_scr, sem, zsem):
    bm = hn_ref.shape[0]
    rows = zero_scr.shape[0]

    @pl.when(pl.program_id(0) == 0)
    def _():
        zero_scr[...] = jnp.zeros(zero_scr.shape, zero_scr.dtype)

        def clear(row0):
            cp = pltpu.make_async_copy(zero_scr, xb_ref.at[pl.ds(pl.multiple_of(row0, rows), rows)], zsem)
            cp.start()
            cp.wait()

        for e in range(N_EXPERTS):
            prev_end = pe_ref[e - 1] if e else 0

            @pl.when(pe_ref[e] > prev_end)
            def _():
                clear(pe_ref[e] - rows)

        def clear_block(b, carry):
            clear(b * rows)
            return carry

        lax.fori_loop(pe_ref[N_EXPERTS - 1] // rows, xb_ref.shape[0] // rows, clear_block, 0)

    def row_copy(r, kk):
        dst = pos_ref[0, 0, TOP_K * r + kk]
        return pltpu.make_async_copy(hn_ref.at[pl.ds(r, 1)], xb_ref.at[pl.ds(dst, 1)], sem)

    def start(r, carry):
        for kk in range(TOP_K):
            row_copy(r, kk).start()
        return carry

    def wait(r, carry):
        for kk in range(TOP_K):
            row_copy(r, kk).wait()
        return carry

    lax.fori_loop(0, bm, start, 0, unroll=8)
    lax.fori_loop(0, bm, wait, 0, unroll=8)


def _dispatch(hnp, pos_blocks, pad_ends, cap, rows):
    t, w = hnp.shape
    nb, _, two_bm = pos_blocks.shape
    bm = two_bm // TOP_K
    grid_spec = pltpu.PrefetchScalarGridSpec(
        num_scalar_prefetch=1,
        grid=(nb,),
        in_specs=[
            pl.BlockSpec((1, 1, two_bm), lambda i, pe: (i, 0, 0), memory_space=pltpu.SMEM),
            pl.BlockSpec((bm, w), lambda i, pe: (i, 0)),
        ],
        out_specs=pl.BlockSpec(memory_space=pl.ANY),
        scratch_shapes=[pltpu.VMEM((rows, w), U32), pltpu.SemaphoreType.DMA(()), pltpu.SemaphoreType.DMA(())],
    )
    return pl.pallas_call(
        _dispatch_kernel,
        grid_spec=grid_spec,
        out_shape=jax.ShapeDtypeStruct((cap, w), U32),
        compiler_params=_cparams("arbitrary"),
        name="moe_dispatch",
    )(pad_ends, pos_blocks, hnp)


def _moe_kernel(be_ref, nu_ref, xb_ref, wg_ref, wu_ref, wd_ref, o_ref, xs_scr):
    del be_ref
    b = pl.program_id(0)
    f = pl.program_id(1)

    @pl.when(f == 0)
    def _():
        o_ref[...] = jnp.zeros(o_ref.shape, F32)

    @pl.when(b < nu_ref[0])
    def _():
        @pl.when(f == 0)
        def _():
            w = xb_ref[...]
            half = w.shape[1]
            xs_scr[:, :half] = pltpu.bitcast(w << 16, F32).astype(BF16)
            xs_scr[:, half:] = pltpu.bitcast(w & jnp.uint32(0xFFFF0000), F32).astype(BF16)

        xs = xs_scr[...]
        h = (_silu(_dot(xs, wg_ref[0])) * _dot(xs, wu_ref[0])).astype(BF16)
        o_ref[...] += _dot(h, wd_ref[0])


def _moe(xb, block_expert, n_used, wg, wu, wd, bm):
    cap, half = xb.shape
    d = 2 * half
    fdim = wg.shape[2]
    bf = _blk(fdim, 1024)
    nf = fdim // bf
    nb = cap // bm

    def live(b, nu):
        return jnp.maximum(jnp.minimum(b, nu[0] - 1), 0)

    def f_eff(b, f, nu):
        return jnp.where(b < nu[0], f, nf - 1)

    grid_spec = pltpu.PrefetchScalarGridSpec(
        num_scalar_prefetch=2,
        grid=(nb, nf),
        in_specs=[
            pl.BlockSpec((bm, half), lambda b, f, be, nu: (live(b, nu), 0)),
            pl.BlockSpec((1, d, bf), lambda b, f, be, nu: (be[live(b, nu)], 0, f_eff(b, f, nu))),
            pl.BlockSpec((1, d, bf), lambda b, f, be, nu: (be[live(b, nu)], 0, f_eff(b, f, nu))),
            pl.BlockSpec((1, bf, d), lambda b, f, be, nu: (be[live(b, nu)], f_eff(b, f, nu), 0)),
        ],
        out_specs=pl.BlockSpec((bm, d), lambda b, f, be, nu: (b, 0)),
        scratch_shapes=[pltpu.VMEM((bm, d), BF16)],
    )
    return pl.pallas_call(
        _moe_kernel,
        grid_spec=grid_spec,
        out_shape=jax.ShapeDtypeStruct((cap, d), F32),
        compiler_params=_cparams("arbitrary", "arbitrary"),
        name="moe_ffn",
    )(block_expert, n_used, xb, wg, wu, wd)


def _combine_kernel(pos_ref, x_ref, gate_ref, g_ref, yb_ref, o_ref, ybuf, sem):
    bm = x_ref.shape[0]

    def row_copy(r, kk):
        src = pos_ref[0, 0, TOP_K * r + kk]
        return pltpu.make_async_copy(yb_ref.at[pl.ds(src, 1)], ybuf.at[kk, pl.ds(r, 1)], sem)

    def start(r, carry):
        for kk in range(TOP_K):
            row_copy(r, kk).start()
        return carry

    def wait(r, carry):
        for kk in range(TOP_K):
            row_copy(r, kk).wait()
        return carry

    lax.fori_loop(0, bm, start, 0, unroll=8)
    lax.fori_loop(0, bm, wait, 0, unroll=8)
    gates = gate_ref[...]
    x = x_ref[...] + gates[:, 0:1] * ybuf[0] + gates[:, 1:2] * ybuf[1]
    o_ref[...] = _rms(x, g_ref[...])


def _combine(x, gates, g_final, yb, pos_blocks):
    t, d = x.shape
    nb, _, two_bm = pos_blocks.shape
    bm = two_bm // TOP_K
    return pl.pallas_call(
        _combine_kernel,
        grid=(nb,),
        in_specs=[
            pl.BlockSpec((1, 1, two_bm), lambda i: (i, 0, 0), memory_space=pltpu.SMEM),
            pl.BlockSpec((bm, d), lambda i: (i, 0)),
            pl.BlockSpec((bm, LANES), lambda i: (i, 0)),
            _resident((1, d), lambda i: (0, 0)),
            pl.BlockSpec(memory_space=pl.ANY),
        ],
        out_specs=pl.BlockSpec((bm, d), lambda i: (i, 0)),
        out_shape=jax.ShapeDtypeStruct((t, d), F32),
        scratch_shapes=[pltpu.VMEM((TOP_K, bm, d), F32), pltpu.SemaphoreType.DMA(())],
        compiler_params=_cparams("arbitrary"),
        name="moe_combine",
    )(pos_blocks, x, gates, g_final, yb)


def _rope_tables(positions):
    inv_freq = ROPE_THETA ** (-jnp.arange(0, MLA_ROPE, 2, dtype=F32) / MLA_ROPE)
    ang = positions.astype(F32).reshape(-1)[:, None] * inv_freq
    cos, sin = jnp.cos(ang), jnp.sin(ang)
    zeros = jnp.zeros((ang.shape[0], LANES - MLA_ROPE), F32)
    return jnp.concatenate([cos, cos, zeros], axis=1), jnp.concatenate([-sin, sin, zeros], axis=1)


def _swap_halves(w):
    half = w.shape[-1] // 2
    return jnp.concatenate([w[..., half:], w[..., :half]], axis=-1)


def _mla_weights(w_in, w_uq, w_ukv):
    d = w_in.shape[0]
    o = MLA_Q_RANK + MLA_KV_RANK
    w_kr = w_in[:, o:]
    zpad = jnp.zeros((d, LANES - MLA_ROPE), w_in.dtype)
    w_in_p = jnp.concatenate([w_in[:, :o], w_kr, zpad, _swap_halves(w_kr), zpad], axis=1).astype(BF16)
    uq = w_uq.reshape(MLA_Q_RANK, MLA_HEADS, MLA_NOPE + MLA_ROPE)
    rope = uq[:, :, MLA_NOPE:]
    zq = jnp.zeros((MLA_Q_RANK, MLA_HEADS, LANES - MLA_ROPE), w_uq.dtype)
    wq = jnp.concatenate([uq[:, :, :MLA_NOPE], rope, zq], axis=2).reshape(MLA_Q_RANK, -1).astype(BF16)
    wqs = jnp.concatenate([_swap_halves(rope), zq], axis=2).reshape(MLA_Q_RANK, -1).astype(BF16)
    ukv = w_ukv.reshape(MLA_KV_RANK, MLA_HEADS, MLA_NOPE + MLA_V)
    wk = ukv[:, :, :MLA_NOPE].reshape(MLA_KV_RANK, -1).astype(BF16)
    wv = ukv[:, :, MLA_NOPE:].reshape(MLA_KV_RANK, -1).astype(BF16)
    return w_in_p, wq, wqs, wk, wv


def kernel(x, positions, mla_w_in, mla_q_norm, mla_kv_norm, mla_w_uq, mla_w_ukv, mla_w_o, ssd_w_in, ssd_conv_w, ssd_conv_b, ssd_dt_bias, ssd_a_log, ssd_d, ssd_norm, ssd_w_o, ffn_w_gate, ffn_w_up, ffn_w_down, moe_router, moe_w_gate, moe_w_up, moe_w_down, norm_mix, norm_ffn, norm_final):
    bsz, s, d = x.shape
    t = bsz * s
    assert norm_mix.shape[0] == 2 and mla_w_in.shape[0] == 1 and ssd_w_in.shape[0] == 1
    xf = x.reshape(t, d)
    cr, sr = _rope_tables(positions)
    row = lambda v: v.reshape(1, -1)
    bf = lambda v: v.astype(BF16)

    w_in_p, wq, wqs, wk, wv = _mla_weights(mla_w_in[0], mla_w_uq[0], mla_w_ukv[0])
    cq, ckv, kr = _mla_in(xf, row(norm_mix[0]), w_in_p, row(mla_q_norm[0]), row(mla_kv_norm[0]), cr, sr)
    q, k, v = _mla_up(cq, ckv, kr, wq, wqs, wk, wv, cr, sr)
    inner = ssd_norm.shape[1]
    heads = inner // SSD_HEADDIM
    n_main = 2 * inner + 2 * SSD_GROUPS * SSD_STATE
    w_in = ssd_w_in[0]
    fdim = ffn_w_gate.shape[2]
    flat = lambda w: w.reshape(-1, w.shape[-1])
    casts = [flat(w) for w in (mla_w_o[0], ffn_w_gate[0], ffn_w_up[0], ffn_w_down[0], ssd_w_o[0],
                               moe_w_gate[0], moe_w_up[0], moe_w_down[0])]
    o, (w_o_b, fg_b, fu_b, fd_b, ssd_o_b, mg_b, mu_b, md_b) = _flash(
        q.reshape(bsz, s, -1), k.reshape(bsz, s, -1), v.reshape(bsz, s, -1), bsz, s, casts)
    ssd_in_b = bf(w_in[:, :n_main])
    xf, hn = _proj_res_norm(o.reshape(t, -1), w_o_b, xf, row(norm_ffn[0]))
    xf = _ffn(hn, xf, fg_b, fu_b, fd_b)

    w_dt_t = bf(jnp.zeros((LANES, d), F32).at[:heads].set(w_in[:, n_main:].T))
    dtb = jnp.zeros((LANES, 1), F32).at[:heads, 0].set(ssd_dt_bias[0])
    zx, dtt = _ssd_in(xf, row(norm_mix[1]), ssd_in_b, w_dt_t, dtb)
    a_log_col = jnp.zeros((LANES, 1), F32).at[:heads, 0].set(ssd_a_log[0])
    d_exp = jnp.repeat(ssd_d[0], SSD_HEADDIM).reshape(1, inner)
    y = _ssd_core(zx, dtt, a_log_col, ssd_conv_w[0], row(ssd_conv_b[0]), d_exp, row(ssd_norm[0]), bsz, s, inner)
    router_p = jnp.zeros((d, LANES), F32).at[:, :N_EXPERTS].set(moe_router[0])
    xf, hnp, gates, meta, cnt = _ssd_out_router(y, ssd_o_b, xf, row(norm_ffn[1]), router_p)

    n_blocks = -(-(t * TOP_K) // MOE_ROWS) + N_EXPERTS
    counts = cnt[0, :N_EXPERTS]
    padded = ((counts + MOE_ROWS - 1) // MOE_ROWS) * MOE_ROWS
    pad_ends = jnp.cumsum(padded)
    pad_starts = pad_ends - padded
    pos = pad_starts[meta[:, 0:TOP_K]] + meta[:, TOP_K:2 * TOP_K]
    n_used = (pad_ends[-1] // MOE_ROWS).astype(I32).reshape(1)
    block_start = jnp.arange(n_blocks, dtype=I32)[:, None] * MOE_ROWS
    block_expert = jnp.minimum(jnp.sum(block_start >= pad_ends[None, :], axis=1), N_EXPERTS - 1).astype(I32)
    bm_d = _blk(t, 512)
    pos_blocks = pos.astype(I32).reshape(t // bm_d, 1, TOP_K * bm_d)
    xb = _dispatch(hnp, pos_blocks, pad_ends.astype(I32), n_blocks * MOE_ROWS, MOE_ROWS)
    yb = _moe(xb, block_expert, n_used, mg_b.reshape(N_EXPERTS, d, fdim), mu_b.reshape(N_EXPERTS, d, fdim),
              md_b.reshape(N_EXPERTS, fdim, d), MOE_ROWS)
    out = _combine(xf, gates, row(norm_final), yb, pos_blocks)
    return out.reshape(bsz, s, d)
```

```python
import functools

import jax
import jax.numpy as jnp
from jax import lax
from jax.experimental import pallas as pl
from jax.experimental.pallas import tpu as pltpu

F32 = jnp.float32
BF16 = jnp.bfloat16
U32 = jnp.uint32
I32 = jnp.int32

RMS_EPS = 1e-6
MLA_HEADS = 16
MLA_Q_RANK = 512
MLA_KV_RANK = 512
MLA_NOPE = 128
MLA_ROPE = 64
MLA_V = 128
ROPE_THETA = 10000.0
SSD_HEADDIM = 64
SSD_GROUPS = 8
SSD_STATE = 128
SSD_CONV = 4
SSD_CHUNK = 256
SSD_SLAB = 128
N_EXPERTS = 8
TOP_K = 2
MOE_ROWS = 512
FLASH_TQ = 1024
FLASH_ROWS = 512

LANES = 128
BF16_SUBLANES = 16
QK_WIDTH = 2 * LANES
VMEM_LIMIT_BYTES = 56 * 1024 * 1024
LOG2_E = 1.4426950408889634


def _cparams(*sem):
    return pltpu.CompilerParams(dimension_semantics=sem, vmem_limit_bytes=VMEM_LIMIT_BYTES)


def _blk(n, pref):
    b = min(n, pref)
    assert n % b == 0, (n, pref)
    return b


def _resident(shape, index_map):
    return pl.BlockSpec(shape, index_map, pipeline_mode=pl.Buffered(1))


def _rms(xf, g):
    r = lax.rsqrt(jnp.mean(xf * xf, axis=-1, keepdims=True) + RMS_EPS)
    return xf * r * g


def _silu(x):
    h = 0.5 * x
    return h + h * jnp.tanh(h)


def _dot(a, b):
    return jnp.dot(a, b, preferred_element_type=F32)


def _dot_nt(a, b, precision=None):
    return lax.dot_general(a, b, (((1,), (1,)), ((), ())), preferred_element_type=F32, precision=precision)


def _dot_tn(a, b):
    return lax.dot_general(a, b, (((0,), (0,)), ((), ())), preferred_element_type=F32)


def _tile_lanes(x, reps):
    return x if reps == 1 else jnp.concatenate([x] * reps, axis=1)


def _split3(x):
    hi = x.astype(BF16).astype(F32)
    r1 = x - hi
    mid = r1.astype(BF16).astype(F32)
    lo = r1 - mid
    return jnp.concatenate([hi, mid, lo, jnp.zeros_like(x)], axis=0).astype(BF16)


def _mla_in_kernel(x_ref, g_ref, w_ref, qn_ref, kvn_ref, cr_ref, sr_ref, cq_ref, ckv_ref, kr_ref):
    hn = _rms(x_ref[...], g_ref[...]).astype(BF16)
    c = _dot(hn, w_ref[...])
    cq_ref[...] = _rms(c[:, :MLA_Q_RANK], qn_ref[...]).astype(BF16)
    ckv_ref[...] = _rms(c[:, MLA_Q_RANK:MLA_Q_RANK + MLA_KV_RANK], kvn_ref[...]).astype(BF16)
    o = MLA_Q_RANK + MLA_KV_RANK
    kr_ref[...] = (c[:, o:o + LANES] * cr_ref[...] + c[:, o + LANES:o + 2 * LANES] * sr_ref[...]).astype(BF16)


def _mla_in(x, g, w_in_p, q_norm, kv_norm, cr, sr):
    t, d = x.shape
    bm = _blk(t, 512)
    wn = w_in_p.shape[1]
    row = lambda i: (i, 0)
    fixed = lambda i: (0, 0)
    return pl.pallas_call(
        _mla_in_kernel,
        grid=(t // bm,),
        in_specs=[
            pl.BlockSpec((bm, d), row),
            _resident((1, d), fixed),
            _resident((d, wn), fixed),
            _resident((1, MLA_Q_RANK), fixed),
            _resident((1, MLA_KV_RANK), fixed),
            pl.BlockSpec((bm, LANES), row),
            pl.BlockSpec((bm, LANES), row),
        ],
        out_specs=[
            pl.BlockSpec((bm, MLA_Q_RANK), row),
            pl.BlockSpec((bm, MLA_KV_RANK), row),
            pl.BlockSpec((bm, LANES), row),
        ],
        out_shape=[
            jax.ShapeDtypeStruct((t, MLA_Q_RANK), BF16),
            jax.ShapeDtypeStruct((t, MLA_KV_RANK), BF16),
            jax.ShapeDtypeStruct((t, LANES), BF16),
        ],
        compiler_params=_cparams("parallel"),
        name="mla_in",
    )(x, g, w_in_p, q_norm, kv_norm, cr, sr)


def _mla_up_kernel(cq_ref, ckv_ref, kr_ref, wq_ref, wqs_ref, wk_ref, wv_ref, cr_ref, sr_ref,
                   q_ref, k_ref, v_ref, *, scale):
    cq = cq_ref[...]
    ckv = ckv_ref[...]
    a = _dot(cq, wq_ref[...])
    b = _dot(cq, wqs_ref[...])
    kn = _dot(ckv, wk_ref[...])
    v_ref[...] = _dot(ckv, wv_ref[...]).astype(BF16)
    cr = cr_ref[...] * scale
    sr = sr_ref[...] * scale
    kr = kr_ref[...]
    for h in range(MLA_HEADS):
        q0 = h * QK_WIDTH
        q_ref[:, q0:q0 + LANES] = (a[:, q0:q0 + LANES] * scale).astype(BF16)
        q_ref[:, q0 + LANES:q0 + QK_WIDTH] = (
            a[:, q0 + LANES:q0 + QK_WIDTH] * cr + b[:, h * LANES:(h + 1) * LANES] * sr).astype(BF16)
        k_ref[:, q0:q0 + LANES] = kn[:, h * LANES:(h + 1) * LANES].astype(BF16)
        k_ref[:, q0 + LANES:q0 + QK_WIDTH] = kr


def _mla_up(cq, ckv, kr, wq, wqs, wk, wv, cr, sr):
    t = cq.shape[0]
    bm = _blk(t, 512)
    row = lambda i: (i, 0)
    fixed = lambda i: (0, 0)
    hq = MLA_HEADS * QK_WIDTH
    hv = MLA_HEADS * MLA_V
    scale = float((MLA_NOPE + MLA_ROPE) ** -0.5) * LOG2_E
    return pl.pallas_call(
        functools.partial(_mla_up_kernel, scale=scale),
        grid=(t // bm,),
        in_specs=[
            pl.BlockSpec((bm, MLA_Q_RANK), row),
            pl.BlockSpec((bm, MLA_KV_RANK), row),
            pl.BlockSpec((bm, LANES), row),
            _resident(wq.shape, fixed),
            _resident(wqs.shape, fixed),
            _resident(wk.shape, fixed),
            _resident(wv.shape, fixed),
            pl.BlockSpec((bm, LANES), row),
            pl.BlockSpec((bm, LANES), row),
        ],
        out_specs=[
            pl.BlockSpec((bm, hq), row),
            pl.BlockSpec((bm, hq), row),
            pl.BlockSpec((bm, hv), row),
        ],
        out_shape=[
            jax.ShapeDtypeStruct((t, hq), BF16),
            jax.ShapeDtypeStruct((t, hq), BF16),
            jax.ShapeDtypeStruct((t, hv), BF16),
        ],
        compiler_params=_cparams("parallel"),
        name="mla_up",
    )(cq, ckv, kr, wq, wqs, wk, wv, cr, sr)


def _flash_kernel(q_ref, k_ref, v_ref, *rest, tq, rows, ncast, cast_from):
    w_refs = rest[:ncast]
    o_ref = rest[ncast]
    wo_refs = rest[ncast + 1:2 * ncast + 1]
    m_scr, l_scr, acc_scr = rest[2 * ncast + 1:]

    i = pl.program_id(2)
    m_scr[...] = jnp.full(m_scr.shape, -jnp.inf, F32)
    l_scr[...] = jnp.zeros(l_scr.shape, F32)
    acc_scr[...] = jnp.zeros(acc_scr.shape, F32)

    def update(row0, nrows, k0, width, masked):
        rs = slice(row0, row0 + nrows)
        s = _dot_nt(q_ref[0, rs, :], k_ref[0, pl.ds(k0, width), :])
        if masked:
            qrel = row0 + lax.broadcasted_iota(I32, (nrows, width), 0)
            krel = lax.broadcasted_iota(I32, (nrows, width), 1)
            s = jnp.where(krel <= qrel, s, -jnp.inf)
        m_prev = m_scr[rs, :]
        m_new = jnp.maximum(m_prev, jnp.max(s, axis=1, keepdims=True))
        alpha = jnp.exp2(m_prev - m_new)
        p = jnp.exp2(s - _tile_lanes(m_new, width // LANES))
        l_scr[rs, :] = alpha * l_scr[rs, :] + jnp.sum(p, axis=1, keepdims=True)
        acc_scr[rs, :] = alpha * acc_scr[rs, :] + _dot(p.astype(BF16), v_ref[0, pl.ds(k0, width), :])
        m_scr[rs, :] = m_new

    def body(j, carry):
        update(0, tq, pl.multiple_of(j * tq, tq), tq, False)
        return carry

    lax.fori_loop(0, i, body, 0)

    @pl.when(i >= cast_from)
    def _():
        for w_ref, wo_ref in zip(w_refs, wo_refs):
            wo_ref[...] = w_ref[...].astype(BF16)

    for r in range(tq // rows):
        update(r * rows, rows, pl.multiple_of(i * tq, tq), (r + 1) * rows, True)
    o_ref[0] = (acc_scr[...] / l_scr[...]).astype(o_ref.dtype)


def _cast_rows(nrows, nsteps):
    for br in range(BF16_SUBLANES, nrows + 1, BF16_SUBLANES):
        if nrows % br == 0 and (nsteps * br) % nrows == 0 and nsteps * br >= nrows:
            return br
    raise ValueError((nrows, nsteps))


def _flash(q, k, v, bsz, s, casts):
    assert MLA_V == LANES
    tq = _blk(s, FLASH_TQ)
    rows = _blk(tq, FLASH_ROWS)
    nq = s // tq
    cast_from = nq // 2
    per_head = nq - cast_from
    nsteps = bsz * MLA_HEADS * per_head
    cast_specs, cast_shapes = [], []
    for w, cols in casts:
        nrows = w.shape[0]
        br = _cast_rows(nrows, nsteps)
        per_block = nsteps * br // nrows

        def index_map(b, h, i, pb=per_block):
            return (((b * MLA_HEADS + h) * per_head + jnp.maximum(i - cast_from, 0)) // pb, 0)

        cast_specs.append(pl.BlockSpec((br, cols), index_map))
        cast_shapes.append(jax.ShapeDtypeStruct((nrows, cols), BF16))
    outs = pl.pallas_call(
        functools.partial(_flash_kernel, tq=tq, rows=rows, ncast=len(casts), cast_from=cast_from),
        grid=(bsz, MLA_HEADS, nq),
        in_specs=[
            pl.BlockSpec((1, tq, QK_WIDTH), lambda b, h, i: (b, i, h)),
            pl.BlockSpec((1, s, QK_WIDTH), lambda b, h, i: (b, 0, h)),
            pl.BlockSpec((1, s, MLA_V), lambda b, h, i: (b, 0, h)),
        ] + cast_specs,
        out_specs=[pl.BlockSpec((1, tq, MLA_V), lambda b, h, i: (b, i, h))] + cast_specs,
        out_shape=[jax.ShapeDtypeStruct((bsz, s, MLA_HEADS * MLA_V), BF16)] + cast_shapes,
        scratch_shapes=[
            pltpu.VMEM((tq, LANES), F32),
            pltpu.VMEM((tq, LANES), F32),
            pltpu.VMEM((tq, MLA_V), F32),
        ],
        compiler_params=_cparams("arbitrary", "arbitrary", "arbitrary"),
        name="mla_flash",
    )(q, k, v, *[w for w, _ in casts])
    return outs[0], outs[1:]


def _proj_res_norm_kernel(a_ref, w_ref, x_ref, g_ref, xo_ref, hn_ref):
    xo = x_ref[...] + _dot(a_ref[...], w_ref[...])
    xo_ref[...] = xo
    hn_ref[...] = _rms(xo, g_ref[...]).astype(hn_ref.dtype)


def _proj_res_norm(a, w, x, g):
    t, kdim = a.shape
    d = w.shape[1]
    bm = _blk(t, 512)
    row = lambda i: (i, 0)
    fixed = lambda i: (0, 0)
    return pl.pallas_call(
        _proj_res_norm_kernel,
        grid=(t // bm,),
        in_specs=[
            pl.BlockSpec((bm, kdim), row),
            _resident((kdim, d), fixed),
            pl.BlockSpec((bm, d), row),
            _resident((1, d), fixed),
        ],
        out_specs=[pl.BlockSpec((bm, d), row), pl.BlockSpec((bm, d), row)],
        out_shape=[
            jax.ShapeDtypeStruct((t, d), F32),
            jax.ShapeDtypeStruct((t, d), BF16),
        ],
        compiler_params=_cparams("parallel"),
        name="proj_res_norm",
    )(a, w, x, g)


def _ffn_kernel(hn_ref, x_ref, wg_ref, wu_ref, wd_ref, o_ref):
    f = pl.program_id(1)

    @pl.when(f == 0)
    def _():
        o_ref[...] = x_ref[...]

    hn = hn_ref[...]
    h = (_silu(_dot(hn, wg_ref[...])) * _dot(hn, wu_ref[...])).astype(BF16)
    o_ref[...] += _dot(h, wd_ref[...])


def _ffn(hn, x, wg, wu, wd):
    t, d = x.shape
    fdim = wg.shape[1]
    bm = _blk(t, 512)
    bf = _blk(fdim, 1024)
    return pl.pallas_call(
        _ffn_kernel,
        grid=(t // bm, fdim // bf),
        in_specs=[
            pl.BlockSpec((bm, d), lambda i, f: (i, 0)),
            pl.BlockSpec((bm, d), lambda i, f: (i, 0)),
            pl.BlockSpec((d, bf), lambda i, f: (0, f)),
            pl.BlockSpec((d, bf), lambda i, f: (0, f)),
            pl.BlockSpec((bf, d), lambda i, f: (f, 0)),
        ],
        out_specs=pl.BlockSpec((bm, d), lambda i, f: (i, 0)),
        out_shape=jax.ShapeDtypeStruct((t, d), F32),
        compiler_params=_cparams("parallel", "arbitrary"),
        name="dense_ffn",
    )(hn, x, wg, wu, wd)


def _ssd_in_kernel(x_ref, g_ref, w_ref, wdt_ref, dtb_ref, zx_ref, dtt_ref, hn_scr):
    j = pl.program_id(1)

    @pl.when(j == 0)
    def _():
        hn = _rms(x_ref[...], g_ref[...]).astype(BF16)
        hn_scr[...] = hn
        raw = _dot_nt(wdt_ref[...], hn) + dtb_ref[...]
        dtt_ref[...] = jnp.maximum(raw, 0.0) + jnp.log1p(jnp.exp(-jnp.abs(raw)))

    zx_ref[...] = _dot(hn_scr[...], w_ref[...]).astype(zx_ref.dtype)


def _ssd_in(x, g, w_main, w_dt_t, dt_bias_col):
    t, d = x.shape
    n = w_main.shape[1]
    bm = _blk(t, 1024)
    bn = _blk(n, 1024)
    hp = w_dt_t.shape[0]
    return pl.pallas_call(
        _ssd_in_kernel,
        grid=(t // bm, n // bn),
        in_specs=[
            pl.BlockSpec((bm, d), lambda i, j: (i, 0)),
            _resident((1, d), lambda i, j: (0, 0)),
            pl.BlockSpec((d, bn), lambda i, j: (0, j)),
            _resident((hp, d), lambda i, j: (0, 0)),
            _resident((hp, 1), lambda i, j: (0, 0)),
        ],
        out_specs=[
            pl.BlockSpec((bm, bn), lambda i, j: (i, j)),
            pl.BlockSpec((hp, bm), lambda i, j: (0, i)),
        ],
        out_shape=[
            jax.ShapeDtypeStruct((t, n), BF16),
            jax.ShapeDtypeStruct((hp, t), F32),
        ],
        scratch_shapes=[pltpu.VMEM((bm, d), BF16)],
        compiler_params=_cparams("parallel", "arbitrary"),
        name="ssd_in",
    )(x, g, w_main, w_dt_t, dt_bias_col)


def _ssd_kernel(z_ref, x_ref, b_ref, c_ref, dtt_ref, alog_ref, cwx_ref, cwb_ref, cwc_ref,
                cbx_ref, cbb_ref, cbc_ref, dsk_ref, ng_ref, y_ref, ubuf, state_scr, *, L, gw):
    c_idx = pl.program_id(2)
    hg = gw // SSD_HEADDIM
    n = SSD_STATE
    tail = 8

    @pl.when(c_idx == 0)
    def _():
        ubuf[0:tail, :] = jnp.zeros((tail, ubuf.shape[1]), F32)
        state_scr[...] = jnp.zeros(state_scr.shape, F32)

    ubuf[tail:tail + L, 0:gw] = x_ref[...].astype(F32)
    ubuf[tail:tail + L, gw:gw + n] = b_ref[...].astype(F32)
    ubuf[tail:tail + L, gw + n:gw + 2 * n] = c_ref[...].astype(F32)

    cw = jnp.concatenate([cwx_ref[...], cwb_ref[...], cwc_ref[...]], axis=1)
    cb = jnp.concatenate([cbx_ref[...], cbb_ref[...], cbc_ref[...]], axis=1)
    conv = cb + cw[SSD_CONV - 1:SSD_CONV, :] * ubuf[tail:tail + L, :]
    for kk in range(SSD_CONV - 1):
        off = tail - (SSD_CONV - 1) + kk
        conv = conv + cw[kk:kk + 1, :] * ubuf[off:off + L, :]
    ubuf[0:tail, :] = ubuf[L:L + tail, :]
    u = _silu(conv)
    xc = u[:, 0:gw]
    bc = u[:, gw:gw + n].astype(BF16)
    cc = u[:, gw + n:gw + 2 * n].astype(BF16)

    dtt = dtt_ref[...]
    a_t = dtt * (-LOG2_E * jnp.exp(alog_ref[...]))
    ri = lax.broadcasted_iota(I32, (L, L), 0)
    ci = lax.broadcasted_iota(I32, (L, L), 1)
    causal = ri >= ci
    tril = jnp.where(causal, 1.0, 0.0).astype(BF16)
    r3 = _dot_nt(_split3(a_t), tril)
    acum_row = r3[0:hg] + r3[hg:2 * hg] + r3[2 * hg:3 * hg]

    def expansion(width):
        hrow = lax.broadcasted_iota(I32, (4 * hg, hg * width), 0)
        hlane = lax.broadcasted_iota(I32, (4 * hg, hg * width), 1) // width
        return jnp.where((hrow % hg == hlane) & (hrow < 3 * hg), 1.0, 0.0).astype(BF16)

    acum3 = _split3(acum_row)
    dt_e = _dot_tn(_split3(dtt), expansion(SSD_HEADDIM))
    acum_e = _dot_tn(acum3, expansion(SSD_HEADDIM))
    acum_c = _dot_tn(acum3, expansion(LANES))
    last_e = acum_e[L - 1:L, :]

    xdt = xc * dt_e
    st = state_scr[...]
    st_b = st.astype(BF16)
    xw = (xdt * jnp.exp2(last_e - acum_e)).astype(BF16)
    state_scr[...] = st * jnp.exp2(last_e) + _dot_tn(bc, xw)

    lane = lax.broadcasted_iota(I32, (L, LANES), 1)
    xhead = []
    for jh in range(hg):
        xp = xdt[:, (jh // 2) * LANES:(jh // 2 + 1) * LANES]
        xhead.append(jnp.where((lane // SSD_HEADDIM) == jh % 2, xp, 0.0).astype(BF16))

    dsk = dsk_ref[...]
    ng = ng_ref[...]
    for r0 in range(0, L, SSD_SLAB):
        rs = slice(r0, r0 + SSD_SLAB)
        cbm = jnp.where(causal[rs, :], _dot_nt(cc[rs, :], bc), 0.0)
        y = _dot(cc[rs, :], st_b) * jnp.exp2(acum_e[rs, :]) + xc[rs, :] * dsk
        pieces = []
        for pair in range(hg // 2):
            yp = None
            for sub in range(2):
                jh = 2 * pair + sub
                col = _tile_lanes(acum_c[rs, jh * LANES:(jh + 1) * LANES], L // LANES)
                seg = jnp.minimum(col - acum_row[jh:jh + 1, :], 0.0)
                d = _dot((cbm * jnp.exp2(seg)).astype(BF16), xhead[jh])
                yp = d if yp is None else yp + d
            pieces.append(yp)
        y = y + jnp.concatenate(pieces, axis=1)
        yg = y * _silu(z_ref[rs, :].astype(F32))
        yg = yg * lax.rsqrt(jnp.mean(yg * yg, axis=-1, keepdims=True) + RMS_EPS)
        y_ref[rs, :] = (yg * ng).astype(y_ref.dtype)


def _ssd_core(zx, dtt, a_log_col, conv_w, conv_b, d_exp, norm_g, bsz, s, inner):
    t = zx.shape[0]
    L = SSD_CHUNK
    assert s % L == 0
    nc = s // L
    gw = inner // SSD_GROUPS
    hg = gw // SSD_HEADDIM
    n = SSD_STATE
    gn = SSD_GROUPS * n
    xb0 = inner // gw
    bb0 = (2 * inner) // n
    cb0 = (2 * inner + gn) // n
    cwb0 = inner // n
    cwc0 = (inner + gn) // n
    rowi = lambda b, g, c: b * nc + c
    return pl.pallas_call(
        functools.partial(_ssd_kernel, L=L, gw=gw),
        grid=(bsz, SSD_GROUPS, nc),
        in_specs=[
            pl.BlockSpec((L, gw), lambda b, g, c: (rowi(b, g, c), g)),
            pl.BlockSpec((L, gw), lambda b, g, c: (rowi(b, g, c), xb0 + g)),
            pl.BlockSpec((L, n), lambda b, g, c: (rowi(b, g, c), bb0 + g)),
            pl.BlockSpec((L, n), lambda b, g, c: (rowi(b, g, c), cb0 + g)),
            pl.BlockSpec((hg, L), lambda b, g, c: (g, rowi(b, g, c))),
            pl.BlockSpec((hg, 1), lambda b, g, c: (g, 0)),
            pl.BlockSpec((SSD_CONV, gw), lambda b, g, c: (0, g)),
            pl.BlockSpec((SSD_CONV, n), lambda b, g, c: (0, cwb0 + g)),
            pl.BlockSpec((SSD_CONV, n), lambda b, g, c: (0, cwc0 + g)),
            pl.BlockSpec((1, gw), lambda b, g, c: (0, g)),
            pl.BlockSpec((1, n), lambda b, g, c: (0, cwb0 + g)),
            pl.BlockSpec((1, n), lambda b, g, c: (0, cwc0 + g)),
            pl.BlockSpec((1, gw), lambda b, g, c: (0, g)),
            pl.BlockSpec((1, gw), lambda b, g, c: (0, g)),
        ],
        out_specs=pl.BlockSpec((L, gw), lambda b, g, c: (rowi(b, g, c), g)),
        out_shape=jax.ShapeDtypeStruct((t, inner), BF16),
        scratch_shapes=[
            pltpu.VMEM((L + 8, gw + 2 * n), F32),
            pltpu.VMEM((n, gw), F32),
        ],
        compiler_params=_cparams("parallel", "parallel", "arbitrary"),
        name="ssd_core",
    )(zx, zx, zx, zx, dtt, a_log_col, conv_w, conv_w, conv_w, conv_b, conv_b, conv_b, d_exp, norm_g)


def _ssd_out_router_kernel(a_ref, w_ref, x_ref, g_ref, r_ref, xo_ref, hnp_ref, gate_ref, meta_ref, cnt_ref,
                           run_scr):
    bm, d = xo_ref.shape

    @pl.when(pl.program_id(0) == 0)
    def _():
        run_scr[...] = jnp.zeros(run_scr.shape, F32)

    xo = x_ref[...] + _dot(a_ref[...], w_ref[...])
    xo_ref[...] = xo

    hn = _rms(xo, g_ref[...])
    hn_hi = hn.astype(BF16)
    bits = pltpu.bitcast(hn_hi.astype(F32), U32)
    hnp_ref[...] = (bits[:, d // 2:] & jnp.uint32(0xFFFF0000)) | (bits[:, :d // 2] >> 16)

    hn_lo = (hn - hn_hi.astype(F32)).astype(BF16)
    rt = r_ref[...]
    rt_hi = rt.astype(BF16)
    rt_lo = (rt - rt_hi.astype(F32)).astype(BF16)
    hi_both = _dot(hn_hi, jnp.concatenate([rt_hi, rt_lo], axis=1))
    logits = hi_both[:, :LANES] + (_dot(hn_lo, rt_hi) + hi_both[:, LANES:])
    lane = lax.broadcasted_iota(I32, logits.shape, 1)
    logits = jnp.where(lane < N_EXPERTS, logits, -jnp.inf)
    m1 = jnp.max(logits, axis=1, keepdims=True)
    i1 = jnp.min(jnp.where(logits == m1, lane, LANES), axis=1, keepdims=True)
    rest = jnp.where(lane == i1, -jnp.inf, logits)
    m2 = jnp.max(rest, axis=1, keepdims=True)
    i2 = jnp.min(jnp.where(rest == m2, lane, LANES), axis=1, keepdims=True)
    e21 = jnp.exp(m2 - m1)
    g1 = 1.0 / (1.0 + e21)
    g2 = e21 / (1.0 + e21)
    gate_ref[...] = jnp.where(lane == 0, g1, jnp.where(lane == 1, g2, 0.0))

    oh1 = lane == i1
    oh2 = lane == i2
    both = (oh1 | oh2).astype(BF16)
    ri = lax.broadcasted_iota(I32, (bm, bm), 0)
    ci = lax.broadcasted_iota(I32, (bm, bm), 1)
    before = _dot((ri > ci).astype(BF16), both) + run_scr[...]
    r1 = jnp.sum(jnp.where(oh1, before, 0.0), axis=1, keepdims=True)
    r2 = jnp.sum(jnp.where(oh2, before, 0.0), axis=1, keepdims=True)
    meta = jnp.where(lane == 0, i1.astype(F32),
                     jnp.where(lane == 1, i2.astype(F32),
                               jnp.where(lane == 2, r1, jnp.where(lane == 3, r2, 0.0))))
    meta_ref[...] = meta.astype(I32)
    run_scr[...] = run_scr[...] + jnp.sum(both.astype(F32), axis=0, keepdims=True)
    cnt_ref[...] = jnp.broadcast_to(run_scr[...], cnt_ref.shape).astype(I32)


def _ssd_out_router(a, w, x, g, router_p):
    t, kdim = a.shape
    d = w.shape[1]
    bm = _blk(t, 256)
    row = lambda i: (i, 0)
    fixed = lambda i: (0, 0)
    return pl.pallas_call(
        _ssd_out_router_kernel,
        grid=(t // bm,),
        in_specs=[
            pl.BlockSpec((bm, kdim), row),
            _resident((kdim, d), fixed),
            pl.BlockSpec((bm, d), row),
            _resident((1, d), fixed),
            _resident((d, LANES), fixed),
        ],
        out_specs=[
            pl.BlockSpec((bm, d), row),
            pl.BlockSpec((bm, d // 2), row),
            pl.BlockSpec((bm, LANES), row),
            pl.BlockSpec((bm, LANES), row),
            pl.BlockSpec((8, LANES), fixed),
        ],
        out_shape=[
            jax.ShapeDtypeStruct((t, d), F32),
            jax.ShapeDtypeStruct((t, d // 2), U32),
            jax.ShapeDtypeStruct((t, LANES), F32),
            jax.ShapeDtypeStruct((t, LANES), I32),
            jax.ShapeDtypeStruct((8, LANES), I32),
        ],
        scratch_shapes=[pltpu.VMEM((1, LANES), F32)],
        compiler_params=_cparams("arbitrary"),
        name="ssd_out_router",
    )(a, w, x, g, router_p)


def _dispatch_kernel(pe_ref, pos_ref, hn_ref, xb_ref, zero_scr, sem, zsem):
    bm = hn_ref.shape[0]
    rows = zero_scr.shape[0]

    @pl.when(pl.program_id(0) == 0)
    def _():
        zero_scr[...] = jnp.zeros(zero_scr.shape, zero_scr.dtype)

        def clear(row0):
            cp = pltpu.make_async_copy(zero_scr, xb_ref.at[pl.ds(pl.multiple_of(row0, rows), rows)], zsem)
            cp.start()
            cp.wait()

        for e in range(N_EXPERTS):
            prev_end = pe_ref[e - 1] if e else 0

            @pl.when(pe_ref[e] > prev_end)
            def _():
                clear(pe_ref[e] - rows)

        def clear_block(b, carry):
            clear(b * rows)
            return carry

        lax.fori_loop(pe_ref[N_EXPERTS - 1] // rows, xb_ref.shape[0] // rows, clear_block, 0)

    def row_copy(r, kk):
        dst = pos_ref[0, 0, TOP_K * r + kk]
        return pltpu.make_async_copy(hn_ref.at[pl.ds(r, 1)], xb_ref.at[pl.ds(dst, 1)], sem)

    def start(r, carry):
        for kk in range(TOP_K):
            row_copy(r, kk).start()
        return carry

    def wait(r, carry):
        for kk in range(TOP_K):
            row_copy(r, kk).wait()
        return carry

    lax.fori_loop(0, bm, start, 0, unroll=8)
    lax.fori_loop(0, bm, wait, 0, unroll=8)


def _dispatch(hnp, pos_blocks, pad_ends, cap, rows):
    t, w = hnp.shape
    nb, _, two_bm = pos_blocks.shape
    bm = two_bm // TOP_K
    grid_spec = pltpu.PrefetchScalarGridSpec(
        num_scalar_prefetch=1,
        grid=(nb,),
        in_specs=[
            pl.BlockSpec((1, 1, two_bm), lambda i, pe: (i, 0, 0), memory_space=pltpu.SMEM),
            pl.BlockSpec((bm, w), lambda i, pe: (i, 0)),
        ],
        out_specs=pl.BlockSpec(memory_space=pl.ANY),
        scratch_shapes=[pltpu.VMEM((rows, w), U32), pltpu.SemaphoreType.DMA(()), pltpu.SemaphoreType.DMA(())],
    )
    return pl.pallas_call(
        _dispatch_kernel,
        grid_spec=grid_spec,
        out_shape=jax.ShapeDtypeStruct((cap, w), U32),
        compiler_params=_cparams("arbitrary"),
        name="moe_dispatch",
    )(pad_ends, pos_blocks, hnp)


def _moe_kernel(be_ref, nu_ref, xb_ref, wg_ref, wu_ref, wd_ref, o_ref, xs_scr):
    del be_ref
    b = pl.program_id(0)
    f = pl.program_id(1)

    @pl.when(f == 0)
    def _():
        o_ref[...] = jnp.zeros(o_ref.shape, F32)

    @pl.when(b < nu_ref[0])
    def _():
        @pl.when(f == 0)
        def _():
            w = xb_ref[...]
            half = w.shape[1]
            xs_scr[:, :half] = pltpu.bitcast(w << 16, F32).astype(BF16)
            xs_scr[:, half:] = pltpu.bitcast(w & jnp.uint32(0xFFFF0000), F32).astype(BF16)

        xs = xs_scr[...]
        h = (_silu(_dot(xs, wg_ref[0])) * _dot(xs, wu_ref[0])).astype(BF16)
        o_ref[...] += _dot(h, wd_ref[0])


def _moe(xb, block_expert, n_used, wg, wu, wd, bm):
    cap, half = xb.shape
    d = 2 * half
    fdim = wg.shape[2]
    bf = _blk(fdim, 1024)
    nf = fdim // bf
    nb = cap // bm

    def live(b, nu):
        return jnp.maximum(jnp.minimum(b, nu[0] - 1), 0)

    def f_eff(b, f, nu):
        return jnp.where(b < nu[0], f, nf - 1)

    grid_spec = pltpu.PrefetchScalarGridSpec(
        num_scalar_prefetch=2,
        grid=(nb, nf),
        in_specs=[
            pl.BlockSpec((bm, half), lambda b, f, be, nu: (live(b, nu), 0)),
            pl.BlockSpec((1, d, bf), lambda b, f, be, nu: (be[live(b, nu)], 0, f_eff(b, f, nu))),
            pl.BlockSpec((1, d, bf), lambda b, f, be, nu: (be[live(b, nu)], 0, f_eff(b, f, nu))),
            pl.BlockSpec((1, bf, d), lambda b, f, be, nu: (be[live(b, nu)], f_eff(b, f, nu), 0)),
        ],
        out_specs=pl.BlockSpec((bm, d), lambda b, f, be, nu: (b, 0)),
        scratch_shapes=[pltpu.VMEM((bm, d), BF16)],
    )
    return pl.pallas_call(
        _moe_kernel,
        grid_spec=grid_spec,
        out_shape=jax.ShapeDtypeStruct((cap, d), F32),
        compiler_params=_cparams("arbitrary", "arbitrary"),
        name="moe_ffn",
    )(block_expert, n_used, xb, wg, wu, wd)


def _combine_kernel(pos_ref, x_ref, gate_ref, g_ref, yb_ref, o_ref, ybuf, sem):
    bm = x_ref.shape[0]

    def row_copy(r, kk):
        src = pos_ref[0, 0, TOP_K * r + kk]
        return pltpu.make_async_copy(yb_ref.at[pl.ds(src, 1)], ybuf.at[kk, pl.ds(r, 1)], sem)

    def start(r, carry):
        for kk in range(TOP_K):
            row_copy(r, kk).start()
        return carry

    def wait(r, carry):
        for kk in range(TOP_K):
            row_copy(r, kk).wait()
        return carry

    lax.fori_loop(0, bm, start, 0, unroll=8)
    lax.fori_loop(0, bm, wait, 0, unroll=8)
    gates = gate_ref[...]
    x = x_ref[...] + gates[:, 0:1] * ybuf[0] + gates[:, 1:2] * ybuf[1]
    o_ref[...] = _rms(x, g_ref[...])


def _combine(x, gates, g_final, yb, pos_blocks):
    t, d = x.shape
    nb, _, two_bm = pos_blocks.shape
    bm = two_bm // TOP_K
    return pl.pallas_call(
        _combine_kernel,
        grid=(nb,),
        in_specs=[
            pl.BlockSpec((1, 1, two_bm), lambda i: (i, 0, 0), memory_space=pltpu.SMEM),
            pl.BlockSpec((bm, d), lambda i: (i, 0)),
            pl.BlockSpec((bm, LANES), lambda i: (i, 0)),
            _resident((1, d), lambda i: (0, 0)),
            pl.BlockSpec(memory_space=pl.ANY),
        ],
        out_specs=pl.BlockSpec((bm, d), lambda i: (i, 0)),
        out_shape=jax.ShapeDtypeStruct((t, d), F32),
        scratch_shapes=[pltpu.VMEM((TOP_K, bm, d), F32), pltpu.SemaphoreType.DMA(())],
        compiler_params=_cparams("arbitrary"),
        name="moe_combine",
    )(pos_blocks, x, gates, g_final, yb)


def _rope_tables(positions):
    inv_freq = ROPE_THETA ** (-jnp.arange(0, MLA_ROPE, 2, dtype=F32) / MLA_ROPE)
    ang = positions.astype(F32).reshape(-1)[:, None] * inv_freq
    cos, sin = jnp.cos(ang), jnp.sin(ang)
    zeros = jnp.zeros((ang.shape[0], LANES - MLA_ROPE), F32)
    return jnp.concatenate([cos, cos, zeros], axis=1), jnp.concatenate([-sin, sin, zeros], axis=1)


def _swap_halves(w):
    half = w.shape[-1] // 2
    return jnp.concatenate([w[..., half:], w[..., :half]], axis=-1)


def _mla_weights(w_in, w_uq, w_ukv):
    d = w_in.shape[0]
    o = MLA_Q_RANK + MLA_KV_RANK
    w_kr = w_in[:, o:]
    zpad = jnp.zeros((d, LANES - MLA_ROPE), w_in.dtype)
    w_in_p = jnp.concatenate([w_in[:, :o], w_kr, zpad, _swap_halves(w_kr), zpad], axis=1).astype(BF16)
    uq = w_uq.reshape(MLA_Q_RANK, MLA_HEADS, MLA_NOPE + MLA_ROPE)
    rope = uq[:, :, MLA_NOPE:]
    zq = jnp.zeros((MLA_Q_RANK, MLA_HEADS, LANES - MLA_ROPE), w_uq.dtype)
    wq = jnp.concatenate([uq[:, :, :MLA_NOPE], rope, zq], axis=2).reshape(MLA_Q_RANK, -1).astype(BF16)
    wqs = jnp.concatenate([_swap_halves(rope), zq], axis=2).reshape(MLA_Q_RANK, -1).astype(BF16)
    ukv = w_ukv.reshape(MLA_KV_RANK, MLA_HEADS, MLA_NOPE + MLA_V)
    wk = ukv[:, :, :MLA_NOPE].reshape(MLA_KV_RANK, -1).astype(BF16)
    wv = ukv[:, :, MLA_NOPE:].reshape(MLA_KV_RANK, -1).astype(BF16)
    return w_in_p, wq, wqs, wk, wv


def kernel(x, positions, mla_w_in, mla_q_norm, mla_kv_norm, mla_w_uq, mla_w_ukv, mla_w_o, ssd_w_in, ssd_conv_w, ssd_conv_b, ssd_dt_bias, ssd_a_log, ssd_d, ssd_norm, ssd_w_o, ffn_w_gate, ffn_w_up, ffn_w_down, moe_router, moe_w_gate, moe_w_up, moe_w_down, norm_mix, norm_ffn, norm_final):
    bsz, s, d = x.shape
    t = bsz * s
    assert norm_mix.shape[0] == 2 and mla_w_in.shape[0] == 1 and ssd_w_in.shape[0] == 1
    xf = x.reshape(t, d)
    cr, sr = _rope_tables(positions)
    row = lambda v: v.reshape(1, -1)
    bf = lambda v: v.astype(BF16)

    w_in_p, wq, wqs, wk, wv = _mla_weights(mla_w_in[0], mla_w_uq[0], mla_w_ukv[0])
    cq, ckv, kr = _mla_in(xf, row(norm_mix[0]), w_in_p, row(mla_q_norm[0]), row(mla_kv_norm[0]), cr, sr)
    q, k, v = _mla_up(cq, ckv, kr, wq, wqs, wk, wv, cr, sr)
    inner = ssd_norm.shape[1]
    heads = inner // SSD_HEADDIM
    n_main = 2 * inner + 2 * SSD_GROUPS * SSD_STATE
    w_in = ssd_w_in[0]
    fdim = ffn_w_gate.shape[2]
    flat = lambda w: w.reshape(-1, w.shape[-1])
    casts = [(flat(w), w.shape[-1]) for w in (mla_w_o[0], ffn_w_gate[0], ffn_w_up[0], ffn_w_down[0], ssd_w_o[0],
                                              moe_w_gate[0], moe_w_up[0], moe_w_down[0])]
    casts.append((w_in, n_main))
    o, (w_o_b, fg_b, fu_b, fd_b, ssd_o_b, mg_b, mu_b, md_b, ssd_in_b) = _flash(
        q.reshape(bsz, s, -1), k.reshape(bsz, s, -1), v.reshape(bsz, s, -1), bsz, s, casts)
    xf, hn = _proj_res_norm(o.reshape(t, -1), w_o_b, xf, row(norm_ffn[0]))
    xf = _ffn(hn, xf, fg_b, fu_b, fd_b)

    w_dt_t = bf(jnp.zeros((LANES, d), F32).at[:heads].set(w_in[:, n_main:].T))
    dtb = jnp.zeros((LANES, 1), F32).at[:heads, 0].set(ssd_dt_bias[0])
    zx, dtt = _ssd_in(xf, row(norm_mix[1]), ssd_in_b, w_dt_t, dtb)
    a_log_col = jnp.zeros((LANES, 1), F32).at[:heads, 0].set(ssd_a_log[0])
    d_exp = jnp.repeat(ssd_d[0], SSD_HEADDIM).reshape(1, inner)
    y = _ssd_core(zx, dtt, a_log_col, ssd_conv_w[0], row(ssd_conv_b[0]), d_exp, row(ssd_norm[0]), bsz, s, inner)
    router_p = jnp.zeros((d, LANES), F32).at[:, :N_EXPERTS].set(moe_router[0])
    xf, hnp, gates, meta, cnt = _ssd_out_router(y, ssd_o_b, xf, row(norm_ffn[1]), router_p)

    n_blocks = -(-(t * TOP_K) // MOE_ROWS) + N_EXPERTS
    counts = cnt[0, :N_EXPERTS]
    padded = ((counts + MOE_ROWS - 1) // MOE_ROWS) * MOE_ROWS
    pad_ends = jnp.cumsum(padded)
    pad_starts = pad_ends - padded
    chosen = meta[:, 0:TOP_K, None] == jnp.arange(N_EXPERTS, dtype=I32)
    pos = jnp.sum(jnp.where(chosen, pad_starts, 0), axis=-1) + meta[:, TOP_K:2 * TOP_K]
    n_used = (pad_ends[-1] // MOE_ROWS).astype(I32).reshape(1)
    block_start = jnp.arange(n_blocks, dtype=I32)[:, None] * MOE_ROWS
    block_expert = jnp.minimum(jnp.sum(block_start >= pad_ends[None, :], axis=1), N_EXPERTS - 1).astype(I32)
    bm_d = _blk(t, 512)
    pos_blocks = pos.astype(I32).reshape(t // bm_d, 1, TOP_K * bm_d)
    xb = _dispatch(hnp, pos_blocks, pad_ends.astype(I32), n_blocks * MOE_ROWS, MOE_ROWS)
    yb = _moe(xb, block_expert, n_used, mg_b.reshape(N_EXPERTS, d, fdim), mu_b.reshape(N_EXPERTS, d, fdim),
              md_b.reshape(N_EXPERTS, fdim, d), MOE_ROWS)
    out = _combine(xf, gates, row(norm_final), yb, pos_blocks)
    return out.reshape(bsz, s, d)
```

```python
import functools

import jax
import jax.numpy as jnp
from jax import lax
from jax.experimental import pallas as pl
from jax.experimental.pallas import tpu as pltpu

F32 = jnp.float32
BF16 = jnp.bfloat16
U32 = jnp.uint32
I32 = jnp.int32

RMS_EPS = 1e-6
MLA_HEADS = 16
MLA_Q_RANK = 512
MLA_KV_RANK = 512
MLA_NOPE = 128
MLA_ROPE = 64
MLA_V = 128
ROPE_THETA = 10000.0
SSD_HEADDIM = 64
SSD_GROUPS = 8
SSD_STATE = 128
SSD_CONV = 4
SSD_CHUNK = 256
SSD_SLAB = 128
N_EXPERTS = 8
TOP_K = 2
MOE_ROWS = 512
FLASH_TQ = 1024
FLASH_ROWS = 512

LANES = 128
BF16_SUBLANES = 16
QK_WIDTH = 2 * LANES
VMEM_LIMIT_BYTES = 56 * 1024 * 1024
LOG2_E = 1.4426950408889634


def _cparams(*sem):
    return pltpu.CompilerParams(dimension_semantics=sem, vmem_limit_bytes=VMEM_LIMIT_BYTES)


def _blk(n, pref):
    b = min(n, pref)
    assert n % b == 0, (n, pref)
    return b


def _resident(shape, index_map):
    return pl.BlockSpec(shape, index_map, pipeline_mode=pl.Buffered(1))


def _rms(xf, g):
    r = lax.rsqrt(jnp.mean(xf * xf, axis=-1, keepdims=True) + RMS_EPS)
    return xf * r * g


def _silu(x):
    h = 0.5 * x
    return h + h * jnp.tanh(h)


def _dot(a, b):
    return jnp.dot(a, b, preferred_element_type=F32)


def _dot_nt(a, b, precision=None):
    return lax.dot_general(a, b, (((1,), (1,)), ((), ())), preferred_element_type=F32, precision=precision)


def _dot_tn(a, b):
    return lax.dot_general(a, b, (((0,), (0,)), ((), ())), preferred_element_type=F32)


def _tile_lanes(x, reps):
    return x if reps == 1 else jnp.concatenate([x] * reps, axis=1)


def _split3(x):
    hi = x.astype(BF16).astype(F32)
    r1 = x - hi
    mid = r1.astype(BF16).astype(F32)
    lo = r1 - mid
    return jnp.concatenate([hi, mid, lo, jnp.zeros_like(x)], axis=0).astype(BF16)


def _mla_in_kernel(x_ref, g_ref, w_ref, qn_ref, kvn_ref, cr_ref, sr_ref, cq_ref, ckv_ref, kr_ref):
    hn = _rms(x_ref[...], g_ref[...]).astype(BF16)
    c = _dot(hn, w_ref[...])
    cq_ref[...] = _rms(c[:, :MLA_Q_RANK], qn_ref[...]).astype(BF16)
    ckv_ref[...] = _rms(c[:, MLA_Q_RANK:MLA_Q_RANK + MLA_KV_RANK], kvn_ref[...]).astype(BF16)
    o = MLA_Q_RANK + MLA_KV_RANK
    kr_ref[...] = (c[:, o:o + LANES] * cr_ref[...] + c[:, o + LANES:o + 2 * LANES] * sr_ref[...]).astype(BF16)


def _mla_in(x, g, w_in_p, q_norm, kv_norm, cr, sr):
    t, d = x.shape
    bm = _blk(t, 512)
    wn = w_in_p.shape[1]
    row = lambda i: (i, 0)
    fixed = lambda i: (0, 0)
    return pl.pallas_call(
        _mla_in_kernel,
        grid=(t // bm,),
        in_specs=[
            pl.BlockSpec((bm, d), row),
            _resident((1, d), fixed),
            _resident((d, wn), fixed),
            _resident((1, MLA_Q_RANK), fixed),
            _resident((1, MLA_KV_RANK), fixed),
            pl.BlockSpec((bm, LANES), row),
            pl.BlockSpec((bm, LANES), row),
        ],
        out_specs=[
            pl.BlockSpec((bm, MLA_Q_RANK), row),
            pl.BlockSpec((bm, MLA_KV_RANK), row),
            pl.BlockSpec((bm, LANES), row),
        ],
        out_shape=[
            jax.ShapeDtypeStruct((t, MLA_Q_RANK), BF16),
            jax.ShapeDtypeStruct((t, MLA_KV_RANK), BF16),
            jax.ShapeDtypeStruct((t, LANES), BF16),
        ],
        compiler_params=_cparams("parallel"),
        name="mla_in",
    )(x, g, w_in_p, q_norm, kv_norm, cr, sr)


def _mla_up_kernel(cq_ref, ckv_ref, kr_ref, wq_ref, wqs_ref, wk_ref, wv_ref, cr_ref, sr_ref,
                   q_ref, k_ref, v_ref, *, scale):
    cq = cq_ref[...]
    ckv = ckv_ref[...]
    a = _dot(cq, wq_ref[...])
    b = _dot(cq, wqs_ref[...])
    kn = _dot(ckv, wk_ref[...])
    v_ref[...] = _dot(ckv, wv_ref[...]).astype(BF16)
    cr = cr_ref[...] * scale
    sr = sr_ref[...] * scale
    kr = kr_ref[...]
    for h in range(MLA_HEADS):
        q0 = h * QK_WIDTH
        q_ref[:, q0:q0 + LANES] = (a[:, q0:q0 + LANES] * scale).astype(BF16)
        q_ref[:, q0 + LANES:q0 + QK_WIDTH] = (
            a[:, q0 + LANES:q0 + QK_WIDTH] * cr + b[:, h * LANES:(h + 1) * LANES] * sr).astype(BF16)
        k_ref[:, q0:q0 + LANES] = kn[:, h * LANES:(h + 1) * LANES].astype(BF16)
        k_ref[:, q0 + LANES:q0 + QK_WIDTH] = kr


def _mla_up(cq, ckv, kr, wq, wqs, wk, wv, cr, sr):
    t = cq.shape[0]
    bm = _blk(t, 512)
    row = lambda i: (i, 0)
    fixed = lambda i: (0, 0)
    hq = MLA_HEADS * QK_WIDTH
    hv = MLA_HEADS * MLA_V
    scale = float((MLA_NOPE + MLA_ROPE) ** -0.5) * LOG2_E
    return pl.pallas_call(
        functools.partial(_mla_up_kernel, scale=scale),
        grid=(t // bm,),
        in_specs=[
            pl.BlockSpec((bm, MLA_Q_RANK), row),
            pl.BlockSpec((bm, MLA_KV_RANK), row),
            pl.BlockSpec((bm, LANES), row),
            _resident(wq.shape, fixed),
            _resident(wqs.shape, fixed),
            _resident(wk.shape, fixed),
            _resident(wv.shape, fixed),
            pl.BlockSpec((bm, LANES), row),
            pl.BlockSpec((bm, LANES), row),
        ],
        out_specs=[
            pl.BlockSpec((bm, hq), row),
            pl.BlockSpec((bm, hq), row),
            pl.BlockSpec((bm, hv), row),
        ],
        out_shape=[
            jax.ShapeDtypeStruct((t, hq), BF16),
            jax.ShapeDtypeStruct((t, hq), BF16),
            jax.ShapeDtypeStruct((t, hv), BF16),
        ],
        compiler_params=_cparams("parallel"),
        name="mla_up",
    )(cq, ckv, kr, wq, wqs, wk, wv, cr, sr)


def _flash_kernel(q_ref, k_ref, v_ref, *rest, tq, rows, cast_tiles):
    ncast = len(cast_tiles)
    w_refs = rest[:ncast]
    o_ref = rest[ncast]
    wo_refs = rest[ncast + 1:2 * ncast + 1]
    m_scr, l_scr, acc_scr = rest[2 * ncast + 1:]

    i = pl.program_id(2)
    m_scr[...] = jnp.full(m_scr.shape, -jnp.inf, F32)
    l_scr[...] = jnp.zeros(l_scr.shape, F32)
    acc_scr[...] = jnp.zeros(acc_scr.shape, F32)

    def update(row0, nrows, k0, width, masked):
        rs = slice(row0, row0 + nrows)
        s = _dot_nt(q_ref[0, rs, :], k_ref[0, pl.ds(k0, width), :])
        if masked:
            qrel = row0 + lax.broadcasted_iota(I32, (nrows, width), 0)
            krel = lax.broadcasted_iota(I32, (nrows, width), 1)
            s = jnp.where(krel <= qrel, s, -jnp.inf)
        m_prev = m_scr[rs, :]
        m_new = jnp.maximum(m_prev, jnp.max(s, axis=1, keepdims=True))
        alpha = jnp.exp2(m_prev - m_new)
        p = jnp.exp2(s - _tile_lanes(m_new, width // LANES))
        l_scr[rs, :] = alpha * l_scr[rs, :] + jnp.sum(p, axis=1, keepdims=True)
        acc_scr[rs, :] = alpha * acc_scr[rs, :] + _dot(p.astype(BF16), v_ref[0, pl.ds(k0, width), :])
        m_scr[rs, :] = m_new

    def body(j, carry):
        update(0, tq, pl.multiple_of(j * tq, tq), tq, False)
        return carry

    lax.fori_loop(0, i, body, 0)

    for first, count in sorted(set(cast_tiles)):
        @pl.when((i >= first) & (i < first + count))
        def _(first=first, count=count):
            for w_ref, wo_ref, tiles in zip(w_refs, wo_refs, cast_tiles):
                if tiles == (first, count):
                    wo_ref[...] = w_ref[...].astype(BF16)

    for r in range(tq // rows):
        update(r * rows, rows, pl.multiple_of(i * tq, tq), (r + 1) * rows, True)
    o_ref[0] = (acc_scr[...] / l_scr[...]).astype(o_ref.dtype)


def _cast_rows(nrows, nsteps):
    for br in range(BF16_SUBLANES, nrows + 1, BF16_SUBLANES):
        if nrows % br == 0 and (nsteps * br) % nrows == 0 and nsteps * br >= nrows:
            return br
    raise ValueError((nrows, nsteps))


def _flash(q, k, v, bsz, s, casts):
    assert MLA_V == LANES
    tq = _blk(s, FLASH_TQ)
    rows = _blk(tq, FLASH_ROWS)
    nq = s // tq
    large_tiles = (nq // 2, nq - nq // 2)
    small_tiles = (0, nq // 2) if nq >= 2 else large_tiles
    cast_specs, cast_shapes, cast_tiles = [], [], []
    for w, cols, is_large in casts:
        first, count = large_tiles if is_large else small_tiles
        nsteps = bsz * MLA_HEADS * count
        nrows = w.shape[0]
        br = _cast_rows(nrows, nsteps)
        per_block = nsteps * br // nrows

        def index_map(b, h, i, pb=per_block, first=first, count=count):
            return (((b * MLA_HEADS + h) * count + jnp.clip(i - first, 0, count - 1)) // pb, 0)

        cast_specs.append(pl.BlockSpec((br, cols), index_map))
        cast_shapes.append(jax.ShapeDtypeStruct((nrows, cols), BF16))
        cast_tiles.append((first, count))
    outs = pl.pallas_call(
        functools.partial(_flash_kernel, tq=tq, rows=rows, cast_tiles=tuple(cast_tiles)),
        grid=(bsz, MLA_HEADS, nq),
        in_specs=[
            pl.BlockSpec((1, tq, QK_WIDTH), lambda b, h, i: (b, i, h)),
            pl.BlockSpec((1, s, QK_WIDTH), lambda b, h, i: (b, 0, h)),
            pl.BlockSpec((1, s, MLA_V), lambda b, h, i: (b, 0, h)),
        ] + cast_specs,
        out_specs=[pl.BlockSpec((1, tq, MLA_V), lambda b, h, i: (b, i, h))] + cast_specs,
        out_shape=[jax.ShapeDtypeStruct((bsz, s, MLA_HEADS * MLA_V), BF16)] + cast_shapes,
        scratch_shapes=[
            pltpu.VMEM((tq, LANES), F32),
            pltpu.VMEM((tq, LANES), F32),
            pltpu.VMEM((tq, MLA_V), F32),
        ],
        compiler_params=_cparams("arbitrary", "arbitrary", "arbitrary"),
        name="mla_flash",
    )(q, k, v, *[c[0] for c in casts])
    return outs[0], outs[1:]


def _proj_res_norm_kernel(a_ref, w_ref, x_ref, g_ref, xo_ref, hn_ref):
    xo = x_ref[...] + _dot(a_ref[...], w_ref[...])
    xo_ref[...] = xo
    hn_ref[...] = _rms(xo, g_ref[...]).astype(hn_ref.dtype)


def _proj_res_norm(a, w, x, g):
    t, kdim = a.shape
    d = w.shape[1]
    bm = _blk(t, 512)
    row = lambda i: (i, 0)
    fixed = lambda i: (0, 0)
    return pl.pallas_call(
        _proj_res_norm_kernel,
        grid=(t // bm,),
        in_specs=[
            pl.BlockSpec((bm, kdim), row),
            _resident((kdim, d), fixed),
            pl.BlockSpec((bm, d), row),
            _resident((1, d), fixed),
        ],
        out_specs=[pl.BlockSpec((bm, d), row), pl.BlockSpec((bm, d), row)],
        out_shape=[
            jax.ShapeDtypeStruct((t, d), F32),
            jax.ShapeDtypeStruct((t, d), BF16),
        ],
        compiler_params=_cparams("parallel"),
        name="proj_res_norm",
    )(a, w, x, g)


def _ffn_kernel(hn_ref, x_ref, wg_ref, wu_ref, wd_ref, o_ref):
    f = pl.program_id(1)

    @pl.when(f == 0)
    def _():
        o_ref[...] = x_ref[...]

    hn = hn_ref[...]
    h = (_silu(_dot(hn, wg_ref[...])) * _dot(hn, wu_ref[...])).astype(BF16)
    o_ref[...] += _dot(h, wd_ref[...])


def _ffn(hn, x, wg, wu, wd):
    t, d = x.shape
    fdim = wg.shape[1]
    bm = _blk(t, 512)
    bf = _blk(fdim, 1024)
    return pl.pallas_call(
        _ffn_kernel,
        grid=(t // bm, fdim // bf),
        in_specs=[
            pl.BlockSpec((bm, d), lambda i, f: (i, 0)),
            pl.BlockSpec((bm, d), lambda i, f: (i, 0)),
            pl.BlockSpec((d, bf), lambda i, f: (0, f)),
            pl.BlockSpec((d, bf), lambda i, f: (0, f)),
            pl.BlockSpec((bf, d), lambda i, f: (f, 0)),
        ],
        out_specs=pl.BlockSpec((bm, d), lambda i, f: (i, 0)),
        out_shape=jax.ShapeDtypeStruct((t, d), F32),
        compiler_params=_cparams("parallel", "arbitrary"),
        name="dense_ffn",
    )(hn, x, wg, wu, wd)


def _ssd_in_kernel(x_ref, g_ref, w_ref, wdt_ref, dtb_ref, zx_ref, dtt_ref, hn_scr):
    j = pl.program_id(1)

    @pl.when(j == 0)
    def _():
        hn = _rms(x_ref[...], g_ref[...]).astype(BF16)
        hn_scr[...] = hn
        raw = _dot_nt(wdt_ref[...], hn) + dtb_ref[...]
        dtt_ref[...] = jnp.maximum(raw, 0.0) + jnp.log1p(jnp.exp(-jnp.abs(raw)))

    zx_ref[...] = _dot(hn_scr[...], w_ref[...]).astype(zx_ref.dtype)


def _ssd_in(x, g, w_main, w_dt_t, dt_bias_col):
    t, d = x.shape
    n = w_main.shape[1]
    bm = _blk(t, 1024)
    bn = _blk(n, 1024)
    hp = w_dt_t.shape[0]
    return pl.pallas_call(
        _ssd_in_kernel,
        grid=(t // bm, n // bn),
        in_specs=[
            pl.BlockSpec((bm, d), lambda i, j: (i, 0)),
            _resident((1, d), lambda i, j: (0, 0)),
            pl.BlockSpec((d, bn), lambda i, j: (0, j)),
            _resident((hp, d), lambda i, j: (0, 0)),
            _resident((hp, 1), lambda i, j: (0, 0)),
        ],
        out_specs=[
            pl.BlockSpec((bm, bn), lambda i, j: (i, j)),
            pl.BlockSpec((hp, bm), lambda i, j: (0, i)),
        ],
        out_shape=[
            jax.ShapeDtypeStruct((t, n), BF16),
            jax.ShapeDtypeStruct((hp, t), F32),
        ],
        scratch_shapes=[pltpu.VMEM((bm, d), BF16)],
        compiler_params=_cparams("parallel", "arbitrary"),
        name="ssd_in",
    )(x, g, w_main, w_dt_t, dt_bias_col)


def _ssd_kernel(z_ref, x_ref, b_ref, c_ref, dtt_ref, alog_ref, cwx_ref, cwb_ref, cwc_ref,
                cbx_ref, cbb_ref, cbc_ref, dsk_ref, ng_ref, y_ref, ubuf, state_scr, *, L, gw):
    c_idx = pl.program_id(2)
    hg = gw // SSD_HEADDIM
    n = SSD_STATE
    tail = 8

    @pl.when(c_idx == 0)
    def _():
        ubuf[0:tail, :] = jnp.zeros((tail, ubuf.shape[1]), F32)
        state_scr[...] = jnp.zeros(state_scr.shape, F32)

    ubuf[tail:tail + L, 0:gw] = x_ref[...].astype(F32)
    ubuf[tail:tail + L, gw:gw + n] = b_ref[...].astype(F32)
    ubuf[tail:tail + L, gw + n:gw + 2 * n] = c_ref[...].astype(F32)

    cw = jnp.concatenate([cwx_ref[...], cwb_ref[...], cwc_ref[...]], axis=1)
    cb = jnp.concatenate([cbx_ref[...], cbb_ref[...], cbc_ref[...]], axis=1)
    conv = cb + cw[SSD_CONV - 1:SSD_CONV, :] * ubuf[tail:tail + L, :]
    for kk in range(SSD_CONV - 1):
        off = tail - (SSD_CONV - 1) + kk
        conv = conv + cw[kk:kk + 1, :] * ubuf[off:off + L, :]
    ubuf[0:tail, :] = ubuf[L:L + tail, :]
    u = _silu(conv)
    xc = u[:, 0:gw]
    bc = u[:, gw:gw + n].astype(BF16)
    cc = u[:, gw + n:gw + 2 * n].astype(BF16)

    dtt = dtt_ref[...]
    a_t = dtt * (-LOG2_E * jnp.exp(alog_ref[...]))
    ri = lax.broadcasted_iota(I32, (L, L), 0)
    ci = lax.broadcasted_iota(I32, (L, L), 1)
    causal = ri >= ci
    tril = jnp.where(causal, 1.0, 0.0).astype(BF16)
    r3 = _dot_nt(_split3(a_t), tril)
    acum_row = r3[0:hg] + r3[hg:2 * hg] + r3[2 * hg:3 * hg]

    def expansion(width):
        hrow = lax.broadcasted_iota(I32, (4 * hg, hg * width), 0)
        hlane = lax.broadcasted_iota(I32, (4 * hg, hg * width), 1) // width
        return jnp.where((hrow % hg == hlane) & (hrow < 3 * hg), 1.0, 0.0).astype(BF16)

    acum3 = _split3(acum_row)
    dt_e = _dot_tn(_split3(dtt), expansion(SSD_HEADDIM))
    acum_e = _dot_tn(acum3, expansion(SSD_HEADDIM))
    acum_c = _dot_tn(acum3, expansion(LANES))
    last_e = acum_e[L - 1:L, :]

    xdt = xc * dt_e
    st = state_scr[...]
    st_b = st.astype(BF16)
    xw = (xdt * jnp.exp2(last_e - acum_e)).astype(BF16)
    state_scr[...] = st * jnp.exp2(last_e) + _dot_tn(bc, xw)

    lane = lax.broadcasted_iota(I32, (L, LANES), 1)
    xhead = []
    for jh in range(hg):
        xp = xdt[:, (jh // 2) * LANES:(jh // 2 + 1) * LANES]
        xhead.append(jnp.where((lane // SSD_HEADDIM) == jh % 2, xp, 0.0).astype(BF16))

    dsk = dsk_ref[...]
    ng = ng_ref[...]
    for r0 in range(0, L, SSD_SLAB):
        rs = slice(r0, r0 + SSD_SLAB)
        cbm = jnp.where(causal[rs, :], _dot_nt(cc[rs, :], bc), 0.0)
        y = _dot(cc[rs, :], st_b) * jnp.exp2(acum_e[rs, :]) + xc[rs, :] * dsk
        pieces = []
        for pair in range(hg // 2):
            yp = None
            for sub in range(2):
                jh = 2 * pair + sub
                col = _tile_lanes(acum_c[rs, jh * LANES:(jh + 1) * LANES], L // LANES)
                seg = jnp.minimum(col - acum_row[jh:jh + 1, :], 0.0)
                d = _dot((cbm * jnp.exp2(seg)).astype(BF16), xhead[jh])
                yp = d if yp is None else yp + d
            pieces.append(yp)
        y = y + jnp.concatenate(pieces, axis=1)
        yg = y * _silu(z_ref[rs, :].astype(F32))
        yg = yg * lax.rsqrt(jnp.mean(yg * yg, axis=-1, keepdims=True) + RMS_EPS)
        y_ref[rs, :] = (yg * ng).astype(y_ref.dtype)


def _ssd_core(zx, dtt, a_log_col, conv_w, conv_b, d_exp, norm_g, bsz, s, inner):
    t = zx.shape[0]
    L = SSD_CHUNK
    assert s % L == 0
    nc = s // L
    gw = inner // SSD_GROUPS
    hg = gw // SSD_HEADDIM
    n = SSD_STATE
    gn = SSD_GROUPS * n
    xb0 = inner // gw
    bb0 = (2 * inner) // n
    cb0 = (2 * inner + gn) // n
    cwb0 = inner // n
    cwc0 = (inner + gn) // n
    rowi = lambda b, g, c: b * nc + c
    return pl.pallas_call(
        functools.partial(_ssd_kernel, L=L, gw=gw),
        grid=(bsz, SSD_GROUPS, nc),
        in_specs=[
            pl.BlockSpec((L, gw), lambda b, g, c: (rowi(b, g, c), g)),
            pl.BlockSpec((L, gw), lambda b, g, c: (rowi(b, g, c), xb0 + g)),
            pl.BlockSpec((L, n), lambda b, g, c: (rowi(b, g, c), bb0 + g)),
            pl.BlockSpec((L, n), lambda b, g, c: (rowi(b, g, c), cb0 + g)),
            pl.BlockSpec((hg, L), lambda b, g, c: (g, rowi(b, g, c))),
            pl.BlockSpec((hg, 1), lambda b, g, c: (g, 0)),
            pl.BlockSpec((SSD_CONV, gw), lambda b, g, c: (0, g)),
            pl.BlockSpec((SSD_CONV, n), lambda b, g, c: (0, cwb0 + g)),
            pl.BlockSpec((SSD_CONV, n), lambda b, g, c: (0, cwc0 + g)),
            pl.BlockSpec((1, gw), lambda b, g, c: (0, g)),
            pl.BlockSpec((1, n), lambda b, g, c: (0, cwb0 + g)),
            pl.BlockSpec((1, n), lambda b, g, c: (0, cwc0 + g)),
            pl.BlockSpec((1, gw), lambda b, g, c: (0, g)),
            pl.BlockSpec((1, gw), lambda b, g, c: (0, g)),
        ],
        out_specs=pl.BlockSpec((L, gw), lambda b, g, c: (rowi(b, g, c), g)),
        out_shape=jax.ShapeDtypeStruct((t, inner), BF16),
        scratch_shapes=[
            pltpu.VMEM((L + 8, gw + 2 * n), F32),
            pltpu.VMEM((n, gw), F32),
        ],
        compiler_params=_cparams("parallel", "parallel", "arbitrary"),
        name="ssd_core",
    )(zx, zx, zx, zx, dtt, a_log_col, conv_w, conv_w, conv_w, conv_b, conv_b, conv_b, d_exp, norm_g)


def _ssd_out_router_kernel(a_ref, w_ref, x_ref, g_ref, r_ref, xo_ref, hnp_ref, gate_ref, meta_ref, cnt_ref,
                           run_scr):
    bm, d = xo_ref.shape

    @pl.when(pl.program_id(0) == 0)
    def _():
        run_scr[...] = jnp.zeros(run_scr.shape, F32)

    xo = x_ref[...] + _dot(a_ref[...], w_ref[...])
    xo_ref[...] = xo

    hn = _rms(xo, g_ref[...])
    hn_hi = hn.astype(BF16)
    bits = pltpu.bitcast(hn_hi.astype(F32), U32)
    hnp_ref[...] = (bits[:, d // 2:] & jnp.uint32(0xFFFF0000)) | (bits[:, :d // 2] >> 16)

    hn_lo = (hn - hn_hi.astype(F32)).astype(BF16)
    rt = r_ref[...]
    rt_hi = rt.astype(BF16)
    rt_lo = (rt - rt_hi.astype(F32)).astype(BF16)
    hi_both = _dot(hn_hi, jnp.concatenate([rt_hi, rt_lo], axis=1))
    logits = hi_both[:, :LANES] + (_dot(hn_lo, rt_hi) + hi_both[:, LANES:])
    lane = lax.broadcasted_iota(I32, logits.shape, 1)
    logits = jnp.where(lane < N_EXPERTS, logits, -jnp.inf)
    m1 = jnp.max(logits, axis=1, keepdims=True)
    i1 = jnp.min(jnp.where(logits == m1, lane, LANES), axis=1, keepdims=True)
    rest = jnp.where(lane == i1, -jnp.inf, logits)
    m2 = jnp.max(rest, axis=1, keepdims=True)
    i2 = jnp.min(jnp.where(rest == m2, lane, LANES), axis=1, keepdims=True)
    e21 = jnp.exp(m2 - m1)
    g1 = 1.0 / (1.0 + e21)
    g2 = e21 / (1.0 + e21)
    gate_ref[...] = jnp.where(lane == 0, g1, jnp.where(lane == 1, g2, 0.0))

    oh1 = lane == i1
    oh2 = lane == i2
    both = (oh1 | oh2).astype(BF16)
    ri = lax.broadcasted_iota(I32, (bm, bm), 0)
    ci = lax.broadcasted_iota(I32, (bm, bm), 1)
    before = _dot((ri > ci).astype(BF16), both) + run_scr[...]
    r1 = jnp.sum(jnp.where(oh1, before, 0.0), axis=1, keepdims=True)
    r2 = jnp.sum(jnp.where(oh2, before, 0.0), axis=1, keepdims=True)
    meta = jnp.where(lane == 0, i1.astype(F32),
                     jnp.where(lane == 1, i2.astype(F32),
                               jnp.where(lane == 2, r1, jnp.where(lane == 3, r2, 0.0))))
    meta_ref[...] = meta.astype(I32)
    run_scr[...] = run_scr[...] + jnp.sum(both.astype(F32), axis=0, keepdims=True)
    cnt_ref[...] = jnp.broadcast_to(run_scr[...], cnt_ref.shape).astype(I32)


def _ssd_out_router(a, w, x, g, router_p):
    t, kdim = a.shape
    d = w.shape[1]
    bm = _blk(t, 256)
    row = lambda i: (i, 0)
    fixed = lambda i: (0, 0)
    return pl.pallas_call(
        _ssd_out_router_kernel,
        grid=(t // bm,),
        in_specs=[
            pl.BlockSpec((bm, kdim), row),
            _resident((kdim, d), fixed),
            pl.BlockSpec((bm, d), row),
            _resident((1, d), fixed),
            _resident((d, LANES), fixed),
        ],
        out_specs=[
            pl.BlockSpec((bm, d), row),
            pl.BlockSpec((bm, d // 2), row),
            pl.BlockSpec((bm, LANES), row),
            pl.BlockSpec((bm, LANES), row),
            pl.BlockSpec((8, LANES), fixed),
        ],
        out_shape=[
            jax.ShapeDtypeStruct((t, d), F32),
            jax.ShapeDtypeStruct((t, d // 2), U32),
            jax.ShapeDtypeStruct((t, LANES), F32),
            jax.ShapeDtypeStruct((t, LANES), I32),
            jax.ShapeDtypeStruct((8, LANES), I32),
        ],
        scratch_shapes=[pltpu.VMEM((1, LANES), F32)],
        compiler_params=_cparams("arbitrary"),
        name="ssd_out_router",
    )(a, w, x, g, router_p)


def _dispatch_kernel(pe_ref, pos_ref, hn_ref, xb_ref, zero_scr, sem, zsem):
    bm = hn_ref.shape[0]
    rows = zero_scr.shape[0]

    @pl.when(pl.program_id(0) == 0)
    def _():
        zero_scr[...] = jnp.zeros(zero_scr.shape, zero_scr.dtype)

        def clear(row0):
            cp = pltpu.make_async_copy(zero_scr, xb_ref.at[pl.ds(pl.multiple_of(row0, rows), rows)], zsem)
            cp.start()
            cp.wait()

        for e in range(N_EXPERTS):
            prev_end = pe_ref[e - 1] if e else 0

            @pl.when(pe_ref[e] > prev_end)
            def _():
                clear(pe_ref[e] - rows)

        def clear_block(b, carry):
            clear(b * rows)
            return carry

        lax.fori_loop(pe_ref[N_EXPERTS - 1] // rows, xb_ref.shape[0] // rows, clear_block, 0)

    def row_copy(r, kk):
        dst = pos_ref[0, 0, TOP_K * r + kk]
        return pltpu.make_async_copy(hn_ref.at[pl.ds(r, 1)], xb_ref.at[pl.ds(dst, 1)], sem)

    def start(r, carry):
        for kk in range(TOP_K):
            row_copy(r, kk).start(priority=kk % 2)
        return carry

    def wait(r, carry):
        for kk in range(TOP_K):
            row_copy(r, kk).wait()
        return carry

    lax.fori_loop(0, bm, start, 0, unroll=8)
    lax.fori_loop(0, bm, wait, 0, unroll=8)


def _dispatch(hnp, pos_blocks, pad_ends, cap, rows):
    t, w = hnp.shape
    nb, _, two_bm = pos_blocks.shape
    bm = two_bm // TOP_K
    grid_spec = pltpu.PrefetchScalarGridSpec(
        num_scalar_prefetch=1,
        grid=(nb,),
        in_specs=[
            pl.BlockSpec((1, 1, two_bm), lambda i, pe: (i, 0, 0), memory_space=pltpu.SMEM),
            pl.BlockSpec((bm, w), lambda i, pe: (i, 0)),
        ],
        out_specs=pl.BlockSpec(memory_space=pl.ANY),
        scratch_shapes=[pltpu.VMEM((rows, w), U32), pltpu.SemaphoreType.DMA(()), pltpu.SemaphoreType.DMA(())],
    )
    return pl.pallas_call(
        _dispatch_kernel,
        grid_spec=grid_spec,
        out_shape=jax.ShapeDtypeStruct((cap, w), U32),
        compiler_params=_cparams("arbitrary"),
        name="moe_dispatch",
    )(pad_ends, pos_blocks, hnp)


def _moe_kernel(be_ref, nu_ref, xb_ref, wg_ref, wu_ref, wd_ref, o_ref, xs_scr):
    del be_ref
    b = pl.program_id(0)
    f = pl.program_id(1)

    @pl.when(f == 0)
    def _():
        o_ref[...] = jnp.zeros(o_ref.shape, F32)

    @pl.when(b < nu_ref[0])
    def _():
        @pl.when(f == 0)
        def _():
            w = xb_ref[...]
            half = w.shape[1]
            xs_scr[:, :half] = pltpu.bitcast(w << 16, F32).astype(BF16)
            xs_scr[:, half:] = pltpu.bitcast(w & jnp.uint32(0xFFFF0000), F32).astype(BF16)

        xs = xs_scr[...]
        h = (_silu(_dot(xs, wg_ref[0])) * _dot(xs, wu_ref[0])).astype(BF16)
        o_ref[...] += _dot(h, wd_ref[0])


def _moe(xb, block_expert, n_used, wg, wu, wd, bm):
    cap, half = xb.shape
    d = 2 * half
    fdim = wg.shape[2]
    bf = _blk(fdim, 1024)
    nf = fdim // bf
    nb = cap // bm

    def live(b, nu):
        return jnp.maximum(jnp.minimum(b, nu[0] - 1), 0)

    def f_eff(b, f, nu):
        return jnp.where(b < nu[0], f, nf - 1)

    grid_spec = pltpu.PrefetchScalarGridSpec(
        num_scalar_prefetch=2,
        grid=(nb, nf),
        in_specs=[
            pl.BlockSpec((bm, half), lambda b, f, be, nu: (live(b, nu), 0)),
            pl.BlockSpec((1, d, bf), lambda b, f, be, nu: (be[live(b, nu)], 0, f_eff(b, f, nu))),
            pl.BlockSpec((1, d, bf), lambda b, f, be, nu: (be[live(b, nu)], 0, f_eff(b, f, nu))),
            pl.BlockSpec((1, bf, d), lambda b, f, be, nu: (be[live(b, nu)], f_eff(b, f, nu), 0)),
        ],
        out_specs=pl.BlockSpec((bm, d), lambda b, f, be, nu: (b, 0)),
        scratch_shapes=[pltpu.VMEM((bm, d), BF16)],
    )
    return pl.pallas_call(
        _moe_kernel,
        grid_spec=grid_spec,
        out_shape=jax.ShapeDtypeStruct((cap, d), F32),
        compiler_params=_cparams("arbitrary", "arbitrary"),
        name="moe_ffn",
    )(block_expert, n_used, xb, wg, wu, wd)


def _combine_kernel(pos_ref, x_ref, gate_ref, g_ref, yb_ref, o_ref, ybuf, sem):
    bm = x_ref.shape[0]

    def row_copy(r, kk):
        src = pos_ref[0, 0, TOP_K * r + kk]
        return pltpu.make_async_copy(yb_ref.at[pl.ds(src, 1)], ybuf.at[kk, pl.ds(r, 1)], sem)

    def start(r, carry):
        for kk in range(TOP_K):
            row_copy(r, kk).start(priority=kk % 2)
        return carry

    def wait(r, carry):
        for kk in range(TOP_K):
            row_copy(r, kk).wait()
        return carry

    lax.fori_loop(0, bm, start, 0, unroll=8)
    lax.fori_loop(0, bm, wait, 0, unroll=8)
    gates = gate_ref[...]
    x = x_ref[...] + gates[:, 0:1] * ybuf[0] + gates[:, 1:2] * ybuf[1]
    o_ref[...] = _rms(x, g_ref[...])


def _combine(x, gates, g_final, yb, pos_blocks):
    t, d = x.shape
    nb, _, two_bm = pos_blocks.shape
    bm = two_bm // TOP_K
    return pl.pallas_call(
        _combine_kernel,
        grid=(nb,),
        in_specs=[
            pl.BlockSpec((1, 1, two_bm), lambda i: (i, 0, 0), memory_space=pltpu.SMEM),
            pl.BlockSpec((bm, d), lambda i: (i, 0)),
            pl.BlockSpec((bm, LANES), lambda i: (i, 0)),
            _resident((1, d), lambda i: (0, 0)),
            pl.BlockSpec(memory_space=pl.ANY),
        ],
        out_specs=pl.BlockSpec((bm, d), lambda i: (i, 0)),
        out_shape=jax.ShapeDtypeStruct((t, d), F32),
        scratch_shapes=[pltpu.VMEM((TOP_K, bm, d), F32), pltpu.SemaphoreType.DMA(())],
        compiler_params=_cparams("arbitrary"),
        name="moe_combine",
    )(pos_blocks, x, gates, g_final, yb)


def _rope_tables(positions):
    inv_freq = ROPE_THETA ** (-jnp.arange(0, MLA_ROPE, 2, dtype=F32) / MLA_ROPE)
    ang = positions.astype(F32).reshape(-1)[:, None] * inv_freq
    cos, sin = jnp.cos(ang), jnp.sin(ang)
    zeros = jnp.zeros((ang.shape[0], LANES - MLA_ROPE), F32)
    return jnp.concatenate([cos, cos, zeros], axis=1), jnp.concatenate([-sin, sin, zeros], axis=1)


def _swap_halves(w):
    half = w.shape[-1] // 2
    return jnp.concatenate([w[..., half:], w[..., :half]], axis=-1)


def _mla_weights(w_in, w_uq, w_ukv):
    d = w_in.shape[0]
    o = MLA_Q_RANK + MLA_KV_RANK
    w_kr = w_in[:, o:]
    zpad = jnp.zeros((d, LANES - MLA_ROPE), w_in.dtype)
    w_in_p = jnp.concatenate([w_in[:, :o], w_kr, zpad, _swap_halves(w_kr), zpad], axis=1).astype(BF16)
    uq = w_uq.reshape(MLA_Q_RANK, MLA_HEADS, MLA_NOPE + MLA_ROPE)
    rope = uq[:, :, MLA_NOPE:]
    zq = jnp.zeros((MLA_Q_RANK, MLA_HEADS, LANES - MLA_ROPE), w_uq.dtype)
    wq = jnp.concatenate([uq[:, :, :MLA_NOPE], rope, zq], axis=2).reshape(MLA_Q_RANK, -1).astype(BF16)
    wqs = jnp.concatenate([_swap_halves(rope), zq], axis=2).reshape(MLA_Q_RANK, -1).astype(BF16)
    ukv = w_ukv.reshape(MLA_KV_RANK, MLA_HEADS, MLA_NOPE + MLA_V)
    wk = ukv[:, :, :MLA_NOPE].reshape(MLA_KV_RANK, -1).astype(BF16)
    wv = ukv[:, :, MLA_NOPE:].reshape(MLA_KV_RANK, -1).astype(BF16)
    return w_in_p, wq, wqs, wk, wv


def kernel(x, positions, mla_w_in, mla_q_norm, mla_kv_norm, mla_w_uq, mla_w_ukv, mla_w_o, ssd_w_in, ssd_conv_w, ssd_conv_b, ssd_dt_bias, ssd_a_log, ssd_d, ssd_norm, ssd_w_o, ffn_w_gate, ffn_w_up, ffn_w_down, moe_router, moe_w_gate, moe_w_up, moe_w_down, norm_mix, norm_ffn, norm_final):
    bsz, s, d = x.shape
    t = bsz * s
    assert norm_mix.shape[0] == 2 and mla_w_in.shape[0] == 1 and ssd_w_in.shape[0] == 1
    xf = x.reshape(t, d)
    cr, sr = _rope_tables(positions)
    row = lambda v: v.reshape(1, -1)
    bf = lambda v: v.astype(BF16)

    w_in_p, wq, wqs, wk, wv = _mla_weights(mla_w_in[0], mla_w_uq[0], mla_w_ukv[0])
    cq, ckv, kr = _mla_in(xf, row(norm_mix[0]), w_in_p, row(mla_q_norm[0]), row(mla_kv_norm[0]), cr, sr)
    q, k, v = _mla_up(cq, ckv, kr, wq, wqs, wk, wv, cr, sr)
    inner = ssd_norm.shape[1]
    heads = inner // SSD_HEADDIM
    n_main = 2 * inner + 2 * SSD_GROUPS * SSD_STATE
    w_in = ssd_w_in[0]
    fdim = ffn_w_gate.shape[2]
    flat = lambda w: w.reshape(-1, w.shape[-1])
    casts = [(flat(w), w.shape[-1], False) for w in (mla_w_o[0], ffn_w_gate[0], ffn_w_up[0], ffn_w_down[0], ssd_w_o[0])]
    casts += [(flat(w), w.shape[-1], True) for w in (moe_w_gate[0], moe_w_up[0], moe_w_down[0])]
    casts.append((w_in, n_main, False))
    o, (w_o_b, fg_b, fu_b, fd_b, ssd_o_b, mg_b, mu_b, md_b, ssd_in_b) = _flash(
        q.reshape(bsz, s, -1), k.reshape(bsz, s, -1), v.reshape(bsz, s, -1), bsz, s, casts)
    xf, hn = _proj_res_norm(o.reshape(t, -1), w_o_b, xf, row(norm_ffn[0]))
    xf = _ffn(hn, xf, fg_b, fu_b, fd_b)

    w_dt_t = bf(jnp.zeros((LANES, d), F32).at[:heads].set(w_in[:, n_main:].T))
    dtb = jnp.zeros((LANES, 1), F32).at[:heads, 0].set(ssd_dt_bias[0])
    zx, dtt = _ssd_in(xf, row(norm_mix[1]), ssd_in_b, w_dt_t, dtb)
    a_log_col = jnp.zeros((LANES, 1), F32).at[:heads, 0].set(ssd_a_log[0])
    d_exp = jnp.repeat(ssd_d[0], SSD_HEADDIM).reshape(1, inner)
    y = _ssd_core(zx, dtt, a_log_col, ssd_conv_w[0], row(ssd_conv_b[0]), d_exp, row(ssd_norm[0]), bsz, s, inner)
    router_p = jnp.zeros((d, LANES), F32).at[:, :N_EXPERTS].set(moe_router[0])
    xf, hnp, gates, meta, cnt = _ssd_out_router(y, ssd_o_b, xf, row(norm_ffn[1]), router_p)

    n_blocks = -(-(t * TOP_K) // MOE_ROWS) + N_EXPERTS
    counts = cnt[0, :N_EXPERTS]
    padded = ((counts + MOE_ROWS - 1) // MOE_ROWS) * MOE_ROWS
    pad_ends = jnp.cumsum(padded)
    pad_starts = pad_ends - padded
    chosen = meta[:, 0:TOP_K, None] == jnp.arange(N_EXPERTS, dtype=I32)
    pos = jnp.sum(jnp.where(chosen, pad_starts, 0), axis=-1) + meta[:, TOP_K:2 * TOP_K]
    n_used = (pad_ends[-1] // MOE_ROWS).astype(I32).reshape(1)
    block_start = jnp.arange(n_blocks, dtype=I32)[:, None] * MOE_ROWS
    block_expert = jnp.minimum(jnp.sum(block_start >= pad_ends[None, :], axis=1), N_EXPERTS - 1).astype(I32)
    bm_d = _blk(t, 512)
    pos_blocks = pos.astype(I32).reshape(t // bm_d, 1, TOP_K * bm_d)
    xb = _dispatch(hnp, pos_blocks, pad_ends.astype(I32), n_blocks * MOE_ROWS, MOE_ROWS)
    yb = _moe(xb, block_expert, n_used, mg_b.reshape(N_EXPERTS, d, fdim), mu_b.reshape(N_EXPERTS, d, fdim),
              md_b.reshape(N_EXPERTS, fdim, d), MOE_ROWS)
    out = _combine(xf, gates, row(norm_final), yb, pos_blocks)
    return out.reshape(bsz, s, d)
```

```python
import functools

import jax
import jax.numpy as jnp
from jax import lax
from jax.experimental import pallas as pl
from jax.experimental.pallas import tpu as pltpu

F32 = jnp.float32
BF16 = jnp.bfloat16
U32 = jnp.uint32
I32 = jnp.int32

RMS_EPS = 1e-6
MLA_HEADS = 16
MLA_Q_RANK = 512
MLA_KV_RANK = 512
MLA_NOPE = 128
MLA_ROPE = 64
MLA_V = 128
ROPE_THETA = 10000.0
SSD_HEADDIM = 64
SSD_GROUPS = 8
SSD_STATE = 128
SSD_CONV = 4
SSD_CHUNK = 256
SSD_SLAB = 128
N_EXPERTS = 8
TOP_K = 2
MOE_ROWS = 512
FLASH_TQ = 1024
FLASH_ROWS = 512

LANES = 128
BF16_SUBLANES = 16
QK_WIDTH = 2 * LANES
VMEM_LIMIT_BYTES = 56 * 1024 * 1024
LOG2_E = 1.4426950408889634


def _cparams(*sem):
    return pltpu.CompilerParams(dimension_semantics=sem, vmem_limit_bytes=VMEM_LIMIT_BYTES)


def _blk(n, pref):
    b = min(n, pref)
    assert n % b == 0, (n, pref)
    return b


def _resident(shape, index_map):
    return pl.BlockSpec(shape, index_map, pipeline_mode=pl.Buffered(1))


def _rms(xf, g):
    r = lax.rsqrt(jnp.mean(xf * xf, axis=-1, keepdims=True) + RMS_EPS)
    return xf * r * g


def _silu(x):
    h = 0.5 * x
    return h + h * jnp.tanh(h)


def _dot(a, b):
    return jnp.dot(a, b, preferred_element_type=F32)


def _dot_nt(a, b, precision=None):
    return lax.dot_general(a, b, (((1,), (1,)), ((), ())), preferred_element_type=F32, precision=precision)


def _dot_tn(a, b):
    return lax.dot_general(a, b, (((0,), (0,)), ((), ())), preferred_element_type=F32)


def _tile_lanes(x, reps):
    return x if reps == 1 else jnp.concatenate([x] * reps, axis=1)


def _split3(x):
    hi = x.astype(BF16).astype(F32)
    r1 = x - hi
    mid = r1.astype(BF16).astype(F32)
    lo = r1 - mid
    return jnp.concatenate([hi, mid, lo, jnp.zeros_like(x)], axis=0).astype(BF16)


def _mla_in_kernel(x_ref, g_ref, w_ref, qn_ref, kvn_ref, cr_ref, sr_ref, cq_ref, ckv_ref, kr_ref):
    hn = _rms(x_ref[...], g_ref[...]).astype(BF16)
    c = _dot(hn, w_ref[...])
    cq_ref[...] = _rms(c[:, :MLA_Q_RANK], qn_ref[...]).astype(BF16)
    ckv_ref[...] = _rms(c[:, MLA_Q_RANK:MLA_Q_RANK + MLA_KV_RANK], kvn_ref[...]).astype(BF16)
    o = MLA_Q_RANK + MLA_KV_RANK
    kr_ref[...] = (c[:, o:o + LANES] * cr_ref[...] + c[:, o + LANES:o + 2 * LANES] * sr_ref[...]).astype(BF16)


def _mla_in(x, g, w_in_p, q_norm, kv_norm, cr, sr):
    t, d = x.shape
    bm = _blk(t, 512)
    wn = w_in_p.shape[1]
    row = lambda i: (i, 0)
    fixed = lambda i: (0, 0)
    return pl.pallas_call(
        _mla_in_kernel,
        grid=(t // bm,),
        in_specs=[
            pl.BlockSpec((bm, d), row),
            _resident((1, d), fixed),
            _resident((d, wn), fixed),
            _resident((1, MLA_Q_RANK), fixed),
            _resident((1, MLA_KV_RANK), fixed),
            pl.BlockSpec((bm, LANES), row),
            pl.BlockSpec((bm, LANES), row),
        ],
        out_specs=[
            pl.BlockSpec((bm, MLA_Q_RANK), row),
            pl.BlockSpec((bm, MLA_KV_RANK), row),
            pl.BlockSpec((bm, LANES), row),
        ],
        out_shape=[
            jax.ShapeDtypeStruct((t, MLA_Q_RANK), BF16),
            jax.ShapeDtypeStruct((t, MLA_KV_RANK), BF16),
            jax.ShapeDtypeStruct((t, LANES), BF16),
        ],
        compiler_params=_cparams("parallel"),
        name="mla_in",
    )(x, g, w_in_p, q_norm, kv_norm, cr, sr)


def _mla_up_kernel(cq_ref, ckv_ref, kr_ref, wq_ref, wqs_ref, wk_ref, wv_ref, cr_ref, sr_ref,
                   q_ref, k_ref, v_ref, *, scale):
    cq = cq_ref[...]
    ckv = ckv_ref[...]
    a = _dot(cq, wq_ref[...])
    b = _dot(cq, wqs_ref[...])
    kn = _dot(ckv, wk_ref[...])
    v_ref[...] = _dot(ckv, wv_ref[...]).astype(BF16)
    cr = cr_ref[...] * scale
    sr = sr_ref[...] * scale
    kr = kr_ref[...]
    for h in range(MLA_HEADS):
        q0 = h * QK_WIDTH
        q_ref[:, q0:q0 + LANES] = (a[:, q0:q0 + LANES] * scale).astype(BF16)
        q_ref[:, q0 + LANES:q0 + QK_WIDTH] = (
            a[:, q0 + LANES:q0 + QK_WIDTH] * cr + b[:, h * LANES:(h + 1) * LANES] * sr).astype(BF16)
        k_ref[:, q0:q0 + LANES] = kn[:, h * LANES:(h + 1) * LANES].astype(BF16)
        k_ref[:, q0 + LANES:q0 + QK_WIDTH] = kr


def _mla_up(cq, ckv, kr, wq, wqs, wk, wv, cr, sr):
    t = cq.shape[0]
    bm = _blk(t, 512)
    row = lambda i: (i, 0)
    fixed = lambda i: (0, 0)
    hq = MLA_HEADS * QK_WIDTH
    hv = MLA_HEADS * MLA_V
    scale = float((MLA_NOPE + MLA_ROPE) ** -0.5) * LOG2_E
    return pl.pallas_call(
        functools.partial(_mla_up_kernel, scale=scale),
        grid=(t // bm,),
        in_specs=[
            pl.BlockSpec((bm, MLA_Q_RANK), row),
            pl.BlockSpec((bm, MLA_KV_RANK), row),
            pl.BlockSpec((bm, LANES), row),
            _resident(wq.shape, fixed),
            _resident(wqs.shape, fixed),
            _resident(wk.shape, fixed),
            _resident(wv.shape, fixed),
            pl.BlockSpec((bm, LANES), row),
            pl.BlockSpec((bm, LANES), row),
        ],
        out_specs=[
            pl.BlockSpec((bm, hq), row),
            pl.BlockSpec((bm, hq), row),
            pl.BlockSpec((bm, hv), row),
        ],
        out_shape=[
            jax.ShapeDtypeStruct((t, hq), BF16),
            jax.ShapeDtypeStruct((t, hq), BF16),
            jax.ShapeDtypeStruct((t, hv), BF16),
        ],
        compiler_params=_cparams("parallel"),
        name="mla_up",
    )(cq, ckv, kr, wq, wqs, wk, wv, cr, sr)


def _flash_kernel(q_ref, k_ref, v_ref, *rest, tq, rows, cast_tiles):
    ncast = len(cast_tiles)
    w_refs = rest[:ncast]
    o_ref = rest[ncast]
    wo_refs = rest[ncast + 1:2 * ncast + 1]
    m_scr, l_scr, acc_scr = rest[2 * ncast + 1:]

    i = pl.program_id(2)
    m_scr[...] = jnp.full(m_scr.shape, -jnp.inf, F32)
    l_scr[...] = jnp.zeros(l_scr.shape, F32)
    acc_scr[...] = jnp.zeros(acc_scr.shape, F32)

    def update(row0, nrows, k0, width, masked):
        rs = slice(row0, row0 + nrows)
        s = _dot_nt(q_ref[0, rs, :], k_ref[0, pl.ds(k0, width), :])
        if masked:
            qrel = row0 + lax.broadcasted_iota(I32, (nrows, width), 0)
            krel = lax.broadcasted_iota(I32, (nrows, width), 1)
            s = jnp.where(krel <= qrel, s, -jnp.inf)
        m_prev = m_scr[rs, :]
        m_new = jnp.maximum(m_prev, jnp.max(s, axis=1, keepdims=True))
        alpha = jnp.exp2(m_prev - m_new)
        p = jnp.exp2(s - _tile_lanes(m_new, width // LANES))
        l_scr[rs, :] = alpha * l_scr[rs, :] + jnp.sum(p, axis=1, keepdims=True)
        acc_scr[rs, :] = alpha * acc_scr[rs, :] + _dot(p.astype(BF16), v_ref[0, pl.ds(k0, width), :])
        m_scr[rs, :] = m_new

    def body(j, carry):
        update(0, tq, pl.multiple_of(j * tq, tq), tq, False)
        return carry

    lax.fori_loop(0, i, body, 0)

    for tile in sorted({t for tiles in cast_tiles for t in tiles}):
        @pl.when(i == tile)
        def _(tile=tile):
            for w_ref, wo_ref, tiles in zip(w_refs, wo_refs, cast_tiles):
                if tile in tiles:
                    wo_ref[...] = w_ref[...].astype(BF16)

    for r in range(tq // rows):
        update(r * rows, rows, pl.multiple_of(i * tq, tq), (r + 1) * rows, True)
    o_ref[0] = (acc_scr[...] / l_scr[...]).astype(o_ref.dtype)


def _cast_rows(nrows, nsteps):
    for br in range(BF16_SUBLANES, nrows + 1, BF16_SUBLANES):
        if nrows % br == 0 and (nsteps * br) % nrows == 0 and nsteps * br >= nrows:
            return br
    raise ValueError((nrows, nsteps))


def _flash(q, k, v, bsz, s, casts):
    assert MLA_V == LANES
    tq = _blk(s, FLASH_TQ)
    rows = _blk(tq, FLASH_ROWS)
    nq = s // tq
    late_tiles = tuple(range(nq // 2, nq))
    spread_tiles = (nq // 2 - 1,) + late_tiles[1:] if nq >= 4 else late_tiles
    cast_specs, cast_shapes, cast_tiles = [], [], []
    for w, cols, spread in casts:
        tiles = spread_tiles if spread else late_tiles
        count = len(tiles)
        nsteps = bsz * MLA_HEADS * count
        nrows = w.shape[0]
        br = _cast_rows(nrows, nsteps)
        per_block = nsteps * br // nrows

        def index_map(b, h, i, pb=per_block, tiles=tiles, count=count):
            done = sum((i >= t).astype(I32) for t in tiles)
            return (((b * MLA_HEADS + h) * count + jnp.maximum(done - 1, 0)) // pb, 0)

        cast_specs.append(pl.BlockSpec((br, cols), index_map))
        cast_shapes.append(jax.ShapeDtypeStruct((nrows, cols), BF16))
        cast_tiles.append(tiles)
    outs = pl.pallas_call(
        functools.partial(_flash_kernel, tq=tq, rows=rows, cast_tiles=tuple(cast_tiles)),
        grid=(bsz, MLA_HEADS, nq),
        in_specs=[
            pl.BlockSpec((1, tq, QK_WIDTH), lambda b, h, i: (b, i, h)),
            pl.BlockSpec((1, s, QK_WIDTH), lambda b, h, i: (b, 0, h)),
            pl.BlockSpec((1, s, MLA_V), lambda b, h, i: (b, 0, h)),
        ] + cast_specs,
        out_specs=[pl.BlockSpec((1, tq, MLA_V), lambda b, h, i: (b, i, h))] + cast_specs,
        out_shape=[jax.ShapeDtypeStruct((bsz, s, MLA_HEADS * MLA_V), BF16)] + cast_shapes,
        scratch_shapes=[
            pltpu.VMEM((tq, LANES), F32),
            pltpu.VMEM((tq, LANES), F32),
            pltpu.VMEM((tq, MLA_V), F32),
        ],
        compiler_params=_cparams("arbitrary", "arbitrary", "arbitrary"),
        name="mla_flash",
    )(q, k, v, *[c[0] for c in casts])
    return outs[0], outs[1:]


def _proj_res_norm_kernel(a_ref, w_ref, x_ref, g_ref, xo_ref, hn_ref):
    xo = x_ref[...] + _dot(a_ref[...], w_ref[...])
    xo_ref[...] = xo
    hn_ref[...] = _rms(xo, g_ref[...]).astype(hn_ref.dtype)


def _proj_res_norm(a, w, x, g):
    t, kdim = a.shape
    d = w.shape[1]
    bm = _blk(t, 512)
    row = lambda i: (i, 0)
    fixed = lambda i: (0, 0)
    return pl.pallas_call(
        _proj_res_norm_kernel,
        grid=(t // bm,),
        in_specs=[
            pl.BlockSpec((bm, kdim), row),
            _resident((kdim, d), fixed),
            pl.BlockSpec((bm, d), row),
            _resident((1, d), fixed),
        ],
        out_specs=[pl.BlockSpec((bm, d), row), pl.BlockSpec((bm, d), row)],
        out_shape=[
            jax.ShapeDtypeStruct((t, d), F32),
            jax.ShapeDtypeStruct((t, d), BF16),
        ],
        compiler_params=_cparams("parallel"),
        name="proj_res_norm",
    )(a, w, x, g)


def _ffn_kernel(hn_ref, x_ref, wg_ref, wu_ref, wd_ref, o_ref):
    f = pl.program_id(1)

    @pl.when(f == 0)
    def _():
        o_ref[...] = x_ref[...]

    hn = hn_ref[...]
    h = (_silu(_dot(hn, wg_ref[...])) * _dot(hn, wu_ref[...])).astype(BF16)
    o_ref[...] += _dot(h, wd_ref[...])


def _ffn(hn, x, wg, wu, wd):
    t, d = x.shape
    fdim = wg.shape[1]
    bm = _blk(t, 512)
    bf = _blk(fdim, 1024)
    return pl.pallas_call(
        _ffn_kernel,
        grid=(t // bm, fdim // bf),
        in_specs=[
            pl.BlockSpec((bm, d), lambda i, f: (i, 0)),
            pl.BlockSpec((bm, d), lambda i, f: (i, 0)),
            pl.BlockSpec((d, bf), lambda i, f: (0, f)),
            pl.BlockSpec((d, bf), lambda i, f: (0, f)),
            pl.BlockSpec((bf, d), lambda i, f: (f, 0)),
        ],
        out_specs=pl.BlockSpec((bm, d), lambda i, f: (i, 0)),
        out_shape=jax.ShapeDtypeStruct((t, d), F32),
        compiler_params=_cparams("parallel", "arbitrary"),
        name="dense_ffn",
    )(hn, x, wg, wu, wd)


def _ssd_in_kernel(x_ref, g_ref, w_ref, wdt_ref, dtb_ref, zx_ref, dtt_ref, hn_scr):
    j = pl.program_id(1)

    @pl.when(j == 0)
    def _():
        hn = _rms(x_ref[...], g_ref[...]).astype(BF16)
        hn_scr[...] = hn
        raw = _dot_nt(wdt_ref[...], hn) + dtb_ref[...]
        dtt_ref[...] = jnp.maximum(raw, 0.0) + jnp.log1p(jnp.exp(-jnp.abs(raw)))

    zx_ref[...] = _dot(hn_scr[...], w_ref[...]).astype(zx_ref.dtype)


def _ssd_in(x, g, w_main, w_dt_t, dt_bias_col):
    t, d = x.shape
    n = w_main.shape[1]
    bm = _blk(t, 1024)
    bn = _blk(n, 1024)
    hp = w_dt_t.shape[0]
    return pl.pallas_call(
        _ssd_in_kernel,
        grid=(t // bm, n // bn),
        in_specs=[
            pl.BlockSpec((bm, d), lambda i, j: (i, 0)),
            _resident((1, d), lambda i, j: (0, 0)),
            pl.BlockSpec((d, bn), lambda i, j: (0, j)),
            _resident((hp, d), lambda i, j: (0, 0)),
            _resident((hp, 1), lambda i, j: (0, 0)),
        ],
        out_specs=[
            pl.BlockSpec((bm, bn), lambda i, j: (i, j)),
            pl.BlockSpec((hp, bm), lambda i, j: (0, i)),
        ],
        out_shape=[
            jax.ShapeDtypeStruct((t, n), BF16),
            jax.ShapeDtypeStruct((hp, t), F32),
        ],
        scratch_shapes=[pltpu.VMEM((bm, d), BF16)],
        compiler_params=_cparams("parallel", "arbitrary"),
        name="ssd_in",
    )(x, g, w_main, w_dt_t, dt_bias_col)


def _ssd_kernel(z_ref, x_ref, b_ref, c_ref, dtt_ref, alog_ref, cwx_ref, cwb_ref, cwc_ref,
                cbx_ref, cbb_ref, cbc_ref, dsk_ref, ng_ref, y_ref, ubuf, state_scr, *, L, gw):
    c_idx = pl.program_id(2)
    hg = gw // SSD_HEADDIM
    n = SSD_STATE
    tail = 8

    @pl.when(c_idx == 0)
    def _():
        ubuf[0:tail, :] = jnp.zeros((tail, ubuf.shape[1]), F32)
        state_scr[...] = jnp.zeros(state_scr.shape, F32)

    ubuf[tail:tail + L, 0:gw] = x_ref[...].astype(F32)
    ubuf[tail:tail + L, gw:gw + n] = b_ref[...].astype(F32)
    ubuf[tail:tail + L, gw + n:gw + 2 * n] = c_ref[...].astype(F32)

    cw = jnp.concatenate([cwx_ref[...], cwb_ref[...], cwc_ref[...]], axis=1)
    cb = jnp.concatenate([cbx_ref[...], cbb_ref[...], cbc_ref[...]], axis=1)
    conv = cb + cw[SSD_CONV - 1:SSD_CONV, :] * ubuf[tail:tail + L, :]
    for kk in range(SSD_CONV - 1):
        off = tail - (SSD_CONV - 1) + kk
        conv = conv + cw[kk:kk + 1, :] * ubuf[off:off + L, :]
    ubuf[0:tail, :] = ubuf[L:L + tail, :]
    u = _silu(conv)
    xc = u[:, 0:gw]
    bc = u[:, gw:gw + n].astype(BF16)
    cc = u[:, gw + n:gw + 2 * n].astype(BF16)

    dtt = dtt_ref[...]
    a_t = dtt * (-LOG2_E * jnp.exp(alog_ref[...]))
    ri = lax.broadcasted_iota(I32, (L, L), 0)
    ci = lax.broadcasted_iota(I32, (L, L), 1)
    causal = ri >= ci
    tril = jnp.where(causal, 1.0, 0.0).astype(BF16)
    r3 = _dot_nt(_split3(a_t), tril)
    acum_row = r3[0:hg] + r3[hg:2 * hg] + r3[2 * hg:3 * hg]

    def expansion(width):
        hrow = lax.broadcasted_iota(I32, (4 * hg, hg * width), 0)
        hlane = lax.broadcasted_iota(I32, (4 * hg, hg * width), 1) // width
        return jnp.where((hrow % hg == hlane) & (hrow < 3 * hg), 1.0, 0.0).astype(BF16)

    acum3 = _split3(acum_row)
    dt_e = _dot_tn(_split3(dtt), expansion(SSD_HEADDIM))
    acum_e = _dot_tn(acum3, expansion(SSD_HEADDIM))
    acum_c = _dot_tn(acum3, expansion(LANES))
    last_e = acum_e[L - 1:L, :]

    xdt = xc * dt_e
    st = state_scr[...]
    st_b = st.astype(BF16)
    xw = (xdt * jnp.exp2(last_e - acum_e)).astype(BF16)
    state_scr[...] = st * jnp.exp2(last_e) + _dot_tn(bc, xw)

    lane = lax.broadcasted_iota(I32, (L, LANES), 1)
    xhead = []
    for jh in range(hg):
        xp = xdt[:, (jh // 2) * LANES:(jh // 2 + 1) * LANES]
        xhead.append(jnp.where((lane // SSD_HEADDIM) == jh % 2, xp, 0.0).astype(BF16))

    dsk = dsk_ref[...]
    ng = ng_ref[...]
    for r0 in range(0, L, SSD_SLAB):
        rs = slice(r0, r0 + SSD_SLAB)
        cbm = jnp.where(causal[rs, :], _dot_nt(cc[rs, :], bc), 0.0)
        y = _dot(cc[rs, :], st_b) * jnp.exp2(acum_e[rs, :]) + xc[rs, :] * dsk
        pieces = []
        for pair in range(hg // 2):
            yp = None
            for sub in range(2):
                jh = 2 * pair + sub
                col = _tile_lanes(acum_c[rs, jh * LANES:(jh + 1) * LANES], L // LANES)
                seg = jnp.minimum(col - acum_row[jh:jh + 1, :], 0.0)
                d = _dot((cbm * jnp.exp2(seg)).astype(BF16), xhead[jh])
                yp = d if yp is None else yp + d
            pieces.append(yp)
        y = y + jnp.concatenate(pieces, axis=1)
        yg = y * _silu(z_ref[rs, :].astype(F32))
        yg = yg * lax.rsqrt(jnp.mean(yg * yg, axis=-1, keepdims=True) + RMS_EPS)
        y_ref[rs, :] = (yg * ng).astype(y_ref.dtype)


def _ssd_core(zx, dtt, a_log_col, conv_w, conv_b, d_exp, norm_g, bsz, s, inner):
    t = zx.shape[0]
    L = SSD_CHUNK
    assert s % L == 0
    nc = s // L
    gw = inner // SSD_GROUPS
    hg = gw // SSD_HEADDIM
    n = SSD_STATE
    gn = SSD_GROUPS * n
    xb0 = inner // gw
    bb0 = (2 * inner) // n
    cb0 = (2 * inner + gn) // n
    cwb0 = inner // n
    cwc0 = (inner + gn) // n
    rowi = lambda b, g, c: b * nc + c
    return pl.pallas_call(
        functools.partial(_ssd_kernel, L=L, gw=gw),
        grid=(bsz, SSD_GROUPS, nc),
        in_specs=[
            pl.BlockSpec((L, gw), lambda b, g, c: (rowi(b, g, c), g)),
            pl.BlockSpec((L, gw), lambda b, g, c: (rowi(b, g, c), xb0 + g)),
            pl.BlockSpec((L, n), lambda b, g, c: (rowi(b, g, c), bb0 + g)),
            pl.BlockSpec((L, n), lambda b, g, c: (rowi(b, g, c), cb0 + g)),
            pl.BlockSpec((hg, L), lambda b, g, c: (g, rowi(b, g, c))),
            pl.BlockSpec((hg, 1), lambda b, g, c: (g, 0)),
            pl.BlockSpec((SSD_CONV, gw), lambda b, g, c: (0, g)),
            pl.BlockSpec((SSD_CONV, n), lambda b, g, c: (0, cwb0 + g)),
            pl.BlockSpec((SSD_CONV, n), lambda b, g, c: (0, cwc0 + g)),
            pl.BlockSpec((1, gw), lambda b, g, c: (0, g)),
            pl.BlockSpec((1, n), lambda b, g, c: (0, cwb0 + g)),
            pl.BlockSpec((1, n), lambda b, g, c: (0, cwc0 + g)),
            pl.BlockSpec((1, gw), lambda b, g, c: (0, g)),
            pl.BlockSpec((1, gw), lambda b, g, c: (0, g)),
        ],
        out_specs=pl.BlockSpec((L, gw), lambda b, g, c: (rowi(b, g, c), g)),
        out_shape=jax.ShapeDtypeStruct((t, inner), BF16),
        scratch_shapes=[
            pltpu.VMEM((L + 8, gw + 2 * n), F32),
            pltpu.VMEM((n, gw), F32),
        ],
        compiler_params=_cparams("parallel", "parallel", "arbitrary"),
        name="ssd_core",
    )(zx, zx, zx, zx, dtt, a_log_col, conv_w, conv_w, conv_w, conv_b, conv_b, conv_b, d_exp, norm_g)


def _ssd_out_router_kernel(a_ref, w_ref, x_ref, g_ref, r_ref, xo_ref, hnp_ref, gate_ref, meta_ref, cnt_ref,
                           run_scr):
    bm, d = xo_ref.shape

    @pl.when(pl.program_id(0) == 0)
    def _():
        run_scr[...] = jnp.zeros(run_scr.shape, F32)

    xo = x_ref[...] + _dot(a_ref[...], w_ref[...])
    xo_ref[...] = xo

    hn = _rms(xo, g_ref[...])
    hn_hi = hn.astype(BF16)
    bits = pltpu.bitcast(hn_hi.astype(F32), U32)
    hnp_ref[...] = (bits[:, d // 2:] & jnp.uint32(0xFFFF0000)) | (bits[:, :d // 2] >> 16)

    hn_lo = (hn - hn_hi.astype(F32)).astype(BF16)
    rt = r_ref[...]
    rt_hi = rt.astype(BF16)
    rt_lo = (rt - rt_hi.astype(F32)).astype(BF16)
    hi_both = _dot(hn_hi, jnp.concatenate([rt_hi, rt_lo], axis=1))
    logits = hi_both[:, :LANES] + (_dot(hn_lo, rt_hi) + hi_both[:, LANES:])
    lane = lax.broadcasted_iota(I32, logits.shape, 1)
    logits = jnp.where(lane < N_EXPERTS, logits, -jnp.inf)
    m1 = jnp.max(logits, axis=1, keepdims=True)
    i1 = jnp.min(jnp.where(logits == m1, lane, LANES), axis=1, keepdims=True)
    rest = jnp.where(lane == i1, -jnp.inf, logits)
    m2 = jnp.max(rest, axis=1, keepdims=True)
    i2 = jnp.min(jnp.where(rest == m2, lane, LANES), axis=1, keepdims=True)
    e21 = jnp.exp(m2 - m1)
    g1 = 1.0 / (1.0 + e21)
    g2 = e21 / (1.0 + e21)
    gate_ref[...] = jnp.where(lane == 0, g1, jnp.where(lane == 1, g2, 0.0))

    oh1 = lane == i1
    oh2 = lane == i2
    both = (oh1 | oh2).astype(BF16)
    ri = lax.broadcasted_iota(I32, (bm, bm), 0)
    ci = lax.broadcasted_iota(I32, (bm, bm), 1)
    before = _dot((ri > ci).astype(BF16), both) + run_scr[...]
    r1 = jnp.sum(jnp.where(oh1, before, 0.0), axis=1, keepdims=True)
    r2 = jnp.sum(jnp.where(oh2, before, 0.0), axis=1, keepdims=True)
    meta = jnp.where(lane == 0, i1.astype(F32),
                     jnp.where(lane == 1, i2.astype(F32),
                               jnp.where(lane == 2, r1, jnp.where(lane == 3, r2, 0.0))))
    meta_ref[...] = meta.astype(I32)
    run_scr[...] = run_scr[...] + jnp.sum(both.astype(F32), axis=0, keepdims=True)
    cnt_ref[...] = jnp.broadcast_to(run_scr[...], cnt_ref.shape).astype(I32)


def _ssd_out_router(a, w, x, g, router_p):
    t, kdim = a.shape
    d = w.shape[1]
    bm = _blk(t, 256)
    row = lambda i: (i, 0)
    fixed = lambda i: (0, 0)
    return pl.pallas_call(
        _ssd_out_router_kernel,
        grid=(t // bm,),
        in_specs=[
            pl.BlockSpec((bm, kdim), row),
            _resident((kdim, d), fixed),
            pl.BlockSpec((bm, d), row),
            _resident((1, d), fixed),
            _resident((d, LANES), fixed),
        ],
        out_specs=[
            pl.BlockSpec((bm, d), row),
            pl.BlockSpec((bm, d // 2), row),
            pl.BlockSpec((bm, LANES), row),
            pl.BlockSpec((bm, LANES), row),
            pl.BlockSpec((8, LANES), fixed),
        ],
        out_shape=[
            jax.ShapeDtypeStruct((t, d), F32),
            jax.ShapeDtypeStruct((t, d // 2), U32),
            jax.ShapeDtypeStruct((t, LANES), F32),
            jax.ShapeDtypeStruct((t, LANES), I32),
            jax.ShapeDtypeStruct((8, LANES), I32),
        ],
        scratch_shapes=[pltpu.VMEM((1, LANES), F32)],
        compiler_params=_cparams("arbitrary"),
        name="ssd_out_router",
    )(a, w, x, g, router_p)


def _dispatch_kernel(pe_ref, pos_ref, hn_ref, xb_ref, zero_scr, sem, zsem):
    bm = hn_ref.shape[0]
    rows = zero_scr.shape[0]

    @pl.when(pl.program_id(0) == 0)
    def _():
        zero_scr[...] = jnp.zeros(zero_scr.shape, zero_scr.dtype)

        def clear(row0):
            cp = pltpu.make_async_copy(zero_scr, xb_ref.at[pl.ds(pl.multiple_of(row0, rows), rows)], zsem)
            cp.start()
            cp.wait()

        for e in range(N_EXPERTS):
            prev_end = pe_ref[e - 1] if e else 0

            @pl.when(pe_ref[e] > prev_end)
            def _():
                clear(pe_ref[e] - rows)

        def clear_block(b, carry):
            clear(b * rows)
            return carry

        lax.fori_loop(pe_ref[N_EXPERTS - 1] // rows, xb_ref.shape[0] // rows, clear_block, 0)

    def row_copy(r, kk):
        dst = pos_ref[0, 0, TOP_K * r + kk]
        return pltpu.make_async_copy(hn_ref.at[pl.ds(r, 1)], xb_ref.at[pl.ds(dst, 1)], sem)

    def start(r, carry):
        for kk in range(TOP_K):
            row_copy(r, kk).start(priority=kk % 2)
        return carry

    def wait(r, carry):
        for kk in range(TOP_K):
            row_copy(r, kk).wait()
        return carry

    lax.fori_loop(0, bm, start, 0, unroll=8)
    lax.fori_loop(0, bm, wait, 0, unroll=8)


def _dispatch(hnp, pos_blocks, pad_ends, cap, rows):
    t, w = hnp.shape
    nb, _, two_bm = pos_blocks.shape
    bm = two_bm // TOP_K
    grid_spec = pltpu.PrefetchScalarGridSpec(
        num_scalar_prefetch=1,
        grid=(nb,),
        in_specs=[
            pl.BlockSpec((1, 1, two_bm), lambda i, pe: (i, 0, 0), memory_space=pltpu.SMEM),
            pl.BlockSpec((bm, w), lambda i, pe: (i, 0)),
        ],
        out_specs=pl.BlockSpec(memory_space=pl.ANY),
        scratch_shapes=[pltpu.VMEM((rows, w), U32), pltpu.SemaphoreType.DMA(()), pltpu.SemaphoreType.DMA(())],
    )
    return pl.pallas_call(
        _dispatch_kernel,
        grid_spec=grid_spec,
        out_shape=jax.ShapeDtypeStruct((cap, w), U32),
        compiler_params=_cparams("arbitrary"),
        name="moe_dispatch",
    )(pad_ends, pos_blocks, hnp)


def _moe_kernel(be_ref, nu_ref, xb_ref, wg_ref, wu_ref, wd_ref, o_ref, xs_scr):
    del be_ref
    b = pl.program_id(0)
    f = pl.program_id(1)

    @pl.when(f == 0)
    def _():
        o_ref[...] = jnp.zeros(o_ref.shape, F32)

    @pl.when(b < nu_ref[0])
    def _():
        @pl.when(f == 0)
        def _():
            w = xb_ref[...]
            half = w.shape[1]
            xs_scr[:, :half] = pltpu.bitcast(w << 16, F32).astype(BF16)
            xs_scr[:, half:] = pltpu.bitcast(w & jnp.uint32(0xFFFF0000), F32).astype(BF16)

        xs = xs_scr[...]
        h = (_silu(_dot(xs, wg_ref[0])) * _dot(xs, wu_ref[0])).astype(BF16)
        o_ref[...] += _dot(h, wd_ref[0])


def _moe(xb, block_expert, n_used, wg, wu, wd, bm):
    cap, half = xb.shape
    d = 2 * half
    fdim = wg.shape[2]
    bf = _blk(fdim, 1024)
    nf = fdim // bf
    nb = cap // bm

    def live(b, nu):
        return jnp.maximum(jnp.minimum(b, nu[0] - 1), 0)

    def f_eff(b, f, nu):
        return jnp.where(b < nu[0], f, nf - 1)

    grid_spec = pltpu.PrefetchScalarGridSpec(
        num_scalar_prefetch=2,
        grid=(nb, nf),
        in_specs=[
            pl.BlockSpec((bm, half), lambda b, f, be, nu: (live(b, nu), 0)),
            pl.BlockSpec((1, d, bf), lambda b, f, be, nu: (be[live(b, nu)], 0, f_eff(b, f, nu))),
            pl.BlockSpec((1, d, bf), lambda b, f, be, nu: (be[live(b, nu)], 0, f_eff(b, f, nu))),
            pl.BlockSpec((1, bf, d), lambda b, f, be, nu: (be[live(b, nu)], f_eff(b, f, nu), 0)),
        ],
        out_specs=pl.BlockSpec((bm, d), lambda b, f, be, nu: (b, 0)),
        scratch_shapes=[pltpu.VMEM((bm, d), BF16)],
    )
    return pl.pallas_call(
        _moe_kernel,
        grid_spec=grid_spec,
        out_shape=jax.ShapeDtypeStruct((cap, d), F32),
        compiler_params=_cparams("arbitrary", "arbitrary"),
        name="moe_ffn",
    )(block_expert, n_used, xb, wg, wu, wd)


def _combine_kernel(pos_ref, x_ref, gate_ref, g_ref, yb_ref, o_ref, ybuf, sem):
    bm = x_ref.shape[0]

    def row_copy(r, kk):
        src = pos_ref[0, 0, TOP_K * r + kk]
        return pltpu.make_async_copy(yb_ref.at[pl.ds(src, 1)], ybuf.at[kk, pl.ds(r, 1)], sem)

    def start(r, carry):
        for kk in range(TOP_K):
            row_copy(r, kk).start(priority=kk % 2)
        return carry

    def wait(r, carry):
        for kk in range(TOP_K):
            row_copy(r, kk).wait()
        return carry

    lax.fori_loop(0, bm, start, 0, unroll=8)
    lax.fori_loop(0, bm, wait, 0, unroll=8)
    gates = gate_ref[...]
    x = x_ref[...] + gates[:, 0:1] * ybuf[0] + gates[:, 1:2] * ybuf[1]
    o_ref[...] = _rms(x, g_ref[...])


def _combine(x, gates, g_final, yb, pos_blocks):
    t, d = x.shape
    nb, _, two_bm = pos_blocks.shape
    bm = two_bm // TOP_K
    return pl.pallas_call(
        _combine_kernel,
        grid=(nb,),
        in_specs=[
            pl.BlockSpec((1, 1, two_bm), lambda i: (i, 0, 0), memory_space=pltpu.SMEM),
            pl.BlockSpec((bm, d), lambda i: (i, 0)),
            pl.BlockSpec((bm, LANES), lambda i: (i, 0)),
            _resident((1, d), lambda i: (0, 0)),
            pl.BlockSpec(memory_space=pl.ANY),
        ],
        out_specs=pl.BlockSpec((bm, d), lambda i: (i, 0)),
        out_shape=jax.ShapeDtypeStruct((t, d), F32),
        scratch_shapes=[pltpu.VMEM((TOP_K, bm, d), F32), pltpu.SemaphoreType.DMA(())],
        compiler_params=_cparams("arbitrary"),
        name="moe_combine",
    )(pos_blocks, x, gates, g_final, yb)


def _rope_tables(positions):
    inv_freq = ROPE_THETA ** (-jnp.arange(0, MLA_ROPE, 2, dtype=F32) / MLA_ROPE)
    ang = positions.astype(F32).reshape(-1)[:, None] * inv_freq
    cos, sin = jnp.cos(ang), jnp.sin(ang)
    zeros = jnp.zeros((ang.shape[0], LANES - MLA_ROPE), F32)
    return jnp.concatenate([cos, cos, zeros], axis=1), jnp.concatenate([-sin, sin, zeros], axis=1)


def _swap_halves(w):
    half = w.shape[-1] // 2
    return jnp.concatenate([w[..., half:], w[..., :half]], axis=-1)


def _mla_weights(w_in, w_uq, w_ukv):
    d = w_in.shape[0]
    o = MLA_Q_RANK + MLA_KV_RANK
    w_kr = w_in[:, o:]
    zpad = jnp.zeros((d, LANES - MLA_ROPE), w_in.dtype)
    w_in_p = jnp.concatenate([w_in[:, :o], w_kr, zpad, _swap_halves(w_kr), zpad], axis=1).astype(BF16)
    uq = w_uq.reshape(MLA_Q_RANK, MLA_HEADS, MLA_NOPE + MLA_ROPE)
    rope = uq[:, :, MLA_NOPE:]
    zq = jnp.zeros((MLA_Q_RANK, MLA_HEADS, LANES - MLA_ROPE), w_uq.dtype)
    wq = jnp.concatenate([uq[:, :, :MLA_NOPE], rope, zq], axis=2).reshape(MLA_Q_RANK, -1).astype(BF16)
    wqs = jnp.concatenate([_swap_halves(rope), zq], axis=2).reshape(MLA_Q_RANK, -1).astype(BF16)
    ukv = w_ukv.reshape(MLA_KV_RANK, MLA_HEADS, MLA_NOPE + MLA_V)
    wk = ukv[:, :, :MLA_NOPE].reshape(MLA_KV_RANK, -1).astype(BF16)
    wv = ukv[:, :, MLA_NOPE:].reshape(MLA_KV_RANK, -1).astype(BF16)
    return w_in_p, wq, wqs, wk, wv


def kernel(x, positions, mla_w_in, mla_q_norm, mla_kv_norm, mla_w_uq, mla_w_ukv, mla_w_o, ssd_w_in, ssd_conv_w, ssd_conv_b, ssd_dt_bias, ssd_a_log, ssd_d, ssd_norm, ssd_w_o, ffn_w_gate, ffn_w_up, ffn_w_down, moe_router, moe_w_gate, moe_w_up, moe_w_down, norm_mix, norm_ffn, norm_final):
    bsz, s, d = x.shape
    t = bsz * s
    assert norm_mix.shape[0] == 2 and mla_w_in.shape[0] == 1 and ssd_w_in.shape[0] == 1
    xf = x.reshape(t, d)
    cr, sr = _rope_tables(positions)
    row = lambda v: v.reshape(1, -1)
    bf = lambda v: v.astype(BF16)

    w_in_p, wq, wqs, wk, wv = _mla_weights(mla_w_in[0], mla_w_uq[0], mla_w_ukv[0])
    cq, ckv, kr = _mla_in(xf, row(norm_mix[0]), w_in_p, row(mla_q_norm[0]), row(mla_kv_norm[0]), cr, sr)
    q, k, v = _mla_up(cq, ckv, kr, wq, wqs, wk, wv, cr, sr)
    inner = ssd_norm.shape[1]
    heads = inner // SSD_HEADDIM
    n_main = 2 * inner + 2 * SSD_GROUPS * SSD_STATE
    w_in = ssd_w_in[0]
    fdim = ffn_w_gate.shape[2]
    flat = lambda w: w.reshape(-1, w.shape[-1])
    casts = [(flat(w), w.shape[-1], False) for w in (mla_w_o[0], ffn_w_gate[0], ffn_w_up[0], ffn_w_down[0], ssd_w_o[0])]
    casts += [(flat(moe_w_gate[0]), fdim, True), (flat(moe_w_up[0]), fdim, False), (flat(moe_w_down[0]), d, False)]
    casts.append((w_in, n_main, False))
    o, (w_o_b, fg_b, fu_b, fd_b, ssd_o_b, mg_b, mu_b, md_b, ssd_in_b) = _flash(
        q.reshape(bsz, s, -1), k.reshape(bsz, s, -1), v.reshape(bsz, s, -1), bsz, s, casts)
    xf, hn = _proj_res_norm(o.reshape(t, -1), w_o_b, xf, row(norm_ffn[0]))
    xf = _ffn(hn, xf, fg_b, fu_b, fd_b)

    w_dt_t = bf(jnp.zeros((LANES, d), F32).at[:heads].set(w_in[:, n_main:].T))
    dtb = jnp.zeros((LANES, 1), F32).at[:heads, 0].set(ssd_dt_bias[0])
    zx, dtt = _ssd_in(xf, row(norm_mix[1]), ssd_in_b, w_dt_t, dtb)
    a_log_col = jnp.zeros((LANES, 1), F32).at[:heads, 0].set(ssd_a_log[0])
    d_exp = jnp.repeat(ssd_d[0], SSD_HEADDIM).reshape(1, inner)
    y = _ssd_core(zx, dtt, a_log_col, ssd_conv_w[0], row(ssd_conv_b[0]), d_exp, row(ssd_norm[0]), bsz, s, inner)
    router_p = jnp.zeros((d, LANES), F32).at[:, :N_EXPERTS].set(moe_router[0])
    xf, hnp, gates, meta, cnt = _ssd_out_router(y, ssd_o_b, xf, row(norm_ffn[1]), router_p)

    n_blocks = -(-(t * TOP_K) // MOE_ROWS) + N_EXPERTS
    counts = cnt[0, :N_EXPERTS]
    padded = ((counts + MOE_ROWS - 1) // MOE_ROWS) * MOE_ROWS
    pad_ends = jnp.cumsum(padded)
    pad_starts = pad_ends - padded
    chosen = meta[:, 0:TOP_K, None] == jnp.arange(N_EXPERTS, dtype=I32)
    pos = jnp.sum(jnp.where(chosen, pad_starts, 0), axis=-1) + meta[:, TOP_K:2 * TOP_K]
    n_used = (pad_ends[-1] // MOE_ROWS).astype(I32).reshape(1)
    block_start = jnp.arange(n_blocks, dtype=I32)[:, None] * MOE_ROWS
    block_expert = jnp.minimum(jnp.sum(block_start >= pad_ends[None, :], axis=1), N_EXPERTS - 1).astype(I32)
    bm_d = _blk(t, 512)
    pos_blocks = pos.astype(I32).reshape(t // bm_d, 1, TOP_K * bm_d)
    xb = _dispatch(hnp, pos_blocks, pad_ends.astype(I32), n_blocks * MOE_ROWS, MOE_ROWS)
    yb = _moe(xb, block_expert, n_used, mg_b.reshape(N_EXPERTS, d, fdim), mu_b.reshape(N_EXPERTS, d, fdim),
              md_b.reshape(N_EXPERTS, fdim, d), MOE_ROWS)
    out = _combine(xf, gates, row(norm_final), yb, pos_blocks)
    return out.reshape(bsz, s, d)
```
